```python
import math
import jax, jax.numpy as jnp
from jax import lax
import numpy as np

D_MODEL = 2048
BATCH = 8
SEQ = 2048
DEPTH = 1

EPS = 1e-6
MEM_LEN = 256

POOL_WIDTH = D_MODEL // 4
POOL_GROUPS = 4
POOL_GROUP_DIM = POOL_WIDTH // POOL_GROUPS
POOL_WINDOWS = (2, 4, 8, 16)

DIFF_HEADS = 8
DIFF_QK_DIM = 64
DIFF_V_DIM = 2 * DIFF_QK_DIM
DIFF_QK_WIDTH = DIFF_HEADS * 2 * DIFF_QK_DIM
DIFF_WIDTH = DIFF_HEADS * DIFF_V_DIM
Q_BLOCK = 128

MEM_HEADS = 4
MEM_HEAD_DIM = 128
MEM_WIDTH = MEM_HEADS * MEM_HEAD_DIM

IN_WIDTH = POOL_WIDTH + 2 * DIFF_QK_WIDTH + DIFF_WIDTH + MEM_WIDTH
IN_SPLITS = (POOL_WIDTH,
             POOL_WIDTH + DIFF_QK_WIDTH,
             POOL_WIDTH + 2 * DIFF_QK_WIDTH,
             POOL_WIDTH + 2 * DIFF_QK_WIDTH + DIFF_WIDTH)
N_BRANCHES = 3

NUM_BUCKETS = 32
MAX_DISTANCE = 128

N_EXPERTS = 32
TOP_K = 4
D_FF = D_MODEL
SWIGLU_LIMIT = 7.0
SWIGLU_ALPHA = 1.702
EXPERT_BLOCK = 256

kernel_name = "hybrid_pool_diffattn_mem_moe"


def rms_norm(x, w):
    xf = x.astype(jnp.float32)
    y = xf * lax.rsqrt(jnp.mean(xf * xf, axis=-1, keepdims=True) + EPS)
    return (y * w.astype(jnp.float32)).astype(x.dtype)


def rel_bucket(rel):
    n = jnp.maximum(rel, 0)
    exact = NUM_BUCKETS // 2
    nf = jnp.maximum(n, 1).astype(jnp.float32)
    large = exact + (jnp.log(nf / exact) / math.log(MAX_DISTANCE / exact)
                     * (NUM_BUCKETS - exact)).astype(jnp.int32)
    large = jnp.minimum(large, NUM_BUCKETS - 1)
    return jnp.where(n < exact, n, large)


def pool_mixer(u, pool_w, pool_scale):
    B, S, _ = u.shape
    uf = u.astype(jnp.float32).reshape(B, S, POOL_GROUPS, POOL_GROUP_DIM)
    csum = lax.cumsum(uf, axis=1)
    t = jnp.arange(S)
    outs = []
    for g, w in enumerate(POOL_WINDOWS):
        c = csum[:, :, g]
        c_prev = jnp.pad(c, ((0, 0), (w, 0), (0, 0)))[:, :S]
        cnt = jnp.minimum(t + 1, w).astype(jnp.float32)[None, :, None]
        outs.append((c - c_prev) / cnt - uf[:, :, g])
    pooled = jnp.stack(outs, axis=2).astype(u.dtype)
    mixed = jnp.einsum('bsgc,gcd->bsgd', pooled, pool_w)
    return mixed.reshape(B, S, POOL_WIDTH) * pool_scale


def diff_attention(q, k, v, rel_bias, lam):
    B, H, _, S, dq = q.shape
    dv = v.shape[-1]
    nblk = S // Q_BLOCK
    scale = 1.0 / math.sqrt(dq)
    qb = q.reshape(B, H, 2, nblk, Q_BLOCK, dq).transpose(3, 0, 1, 2, 4, 5)
    k_pos = jnp.arange(S)

    def one_block(args):
        qi, blk = args
        q_pos = blk * Q_BLOCK + jnp.arange(Q_BLOCK)
        rel = q_pos[:, None] - k_pos[None, :]
        bias = rel_bias[rel_bucket(rel)].astype(jnp.float32).transpose(2, 0, 1)
        logits = jnp.einsum('bhmqd,bhmkd->bhmqk', qi, k).astype(jnp.float32) * scale
        logits = logits + bias[None, :, None]
        logits = jnp.where((rel >= 0)[None, None, None], logits, -jnp.inf)
        p = jax.nn.softmax(logits, axis=-1)
        a = p[:, :, 0] - lam * p[:, :, 1]
        return jnp.einsum('bhqk,bhkd->bhqd', a.astype(v.dtype), v)

    out = lax.map(one_block, (qb, jnp.arange(nblk)))
    return out.transpose(1, 2, 0, 3, 4).reshape(B, H, S, dv)


def clamped_swiglu(gu):
    gate, up = gu[..., :D_FF], gu[..., D_FF:]
    gate = jnp.minimum(gate, SWIGLU_LIMIT)
    up = jnp.clip(up, -SWIGLU_LIMIT, SWIGLU_LIMIT)
    return gate * jax.nn.sigmoid(gate * SWIGLU_ALPHA) * (up + 1.0)


def moe_ffn(h, router_w, router_b, w_gu, b_gu, w_down, b_down):
    T, D = h.shape
    logits = (h @ router_w).astype(jnp.float32) + router_b.astype(jnp.float32)
    top_vals, top_idx = lax.top_k(logits, TOP_K)
    gates = jax.nn.softmax(top_vals, axis=-1)
    TK = T * TOP_K
    flat_e = top_idx.reshape(TK)
    flat_tok = jnp.repeat(jnp.arange(T, dtype=jnp.int32), TOP_K)
    flat_g = gates.reshape(TK)
    order = jnp.argsort(flat_e)
    e_sorted = flat_e[order]
    tok_sorted = flat_tok[order]
    g_sorted = flat_g[order]
    counts = jnp.zeros((N_EXPERTS,), jnp.int32).at[flat_e].add(1)
    padded = ((counts + EXPERT_BLOCK - 1) // EXPERT_BLOCK) * EXPERT_BLOCK
    start = jnp.cumsum(counts) - counts
    pcum = jnp.cumsum(padded)
    pstart = pcum - padded
    dest = pstart[e_sorted] + (jnp.arange(TK, dtype=jnp.int32) - start[e_sorted])
    n_blocks = -(-TK // EXPERT_BLOCK) + N_EXPERTS
    P = n_blocks * EXPERT_BLOCK
    row_tok = jnp.full((P,), T, jnp.int32).at[dest].set(tok_sorted)
    block_start = jnp.arange(n_blocks, dtype=jnp.int32) * EXPERT_BLOCK
    block_expert = jnp.minimum(jnp.searchsorted(pcum, block_start, side='right'),
                               N_EXPERTS - 1)
    h_pad = jnp.concatenate([h, jnp.zeros((1, D), h.dtype)], axis=0)
    xb = h_pad[row_tok].reshape(n_blocks, EXPERT_BLOCK, D)

    def expert_block(args):
        xe, e = args
        gu = xe @ w_gu[e] + b_gu[e]
        return clamped_swiglu(gu) @ w_down[e] + b_down[e]

    yb = lax.map(expert_block, (xb, block_expert)).reshape(P, D)
    y_rows = yb[dest] * g_sorted[:, None].astype(yb.dtype)
    return jax.ops.segment_sum(y_rows, tok_sorted, num_segments=T)


def setup_inputs(seed: int = 0) -> dict:
    key = jax.random.key(seed)
    ks = iter(jax.random.split(key, 64))
    L = DEPTH

    def nrm(shape, scale):
        return jax.random.normal(next(ks), shape, jnp.float32) * scale

    def gain(shape):
        return 1.0 + 0.02 * jax.random.normal(next(ks), shape, jnp.float32)

    return {
        "x": nrm((BATCH, SEQ, D_MODEL), 1.0),
        "mem": nrm((BATCH, MEM_LEN, D_MODEL), 1.0),
        "norm_mix_w": gain((L, D_MODEL)),
        "norm_mem_w": gain((L, D_MODEL)),
        "w_in": nrm((L, D_MODEL, IN_WIDTH), D_MODEL ** -0.5),
        "w_gate": nrm((L, D_MODEL, N_BRANCHES * D_MODEL), D_MODEL ** -0.5),
        "b_gate": nrm((L, N_BRANCHES * D_MODEL), 0.02),
        "pool_w": nrm((L, POOL_GROUPS, POOL_GROUP_DIM, POOL_GROUP_DIM), POOL_GROUP_DIM ** -0.5),
        "pool_scale": gain((L, POOL_WIDTH)),
        "q_norm_w": gain((L, DIFF_QK_DIM)),
        "k_norm_w": gain((L, DIFF_QK_DIM)),
        "lambda_q1": nrm((L, DIFF_QK_DIM), 0.1),
        "lambda_k1": nrm((L, DIFF_QK_DIM), 0.1),
        "lambda_q2": nrm((L, DIFF_QK_DIM), 0.1),
        "lambda_k2": nrm((L, DIFF_QK_DIM), 0.1),
        "diff_norm_w": gain((L, DIFF_V_DIM)),
        "rel_bias": nrm((NUM_BUCKETS, DIFF_HEADS), 0.5),
        "w_mem_kv": nrm((L, D_MODEL, 2 * MEM_WIDTH), D_MODEL ** -0.5),
        "mem_q_norm_w": gain((L, MEM_HEAD_DIM)),
        "mem_k_norm_w": gain((L, MEM_HEAD_DIM)),
        "w_up_pool": nrm((L, POOL_WIDTH, D_MODEL), POOL_WIDTH ** -0.5),
        "w_up_diff": nrm((L, DIFF_WIDTH, D_MODEL), DIFF_WIDTH ** -0.5),
        "w_up_mem": nrm((L, MEM_WIDTH, D_MODEL), MEM_WIDTH ** -0.5),
        "w_out": nrm((L, D_MODEL, D_MODEL), D_MODEL ** -0.5),
        "norm_ffn_w": gain((L, D_MODEL)),
        "router_w": nrm((L, D_MODEL, N_EXPERTS), D_MODEL ** -0.5),
        "router_b": nrm((L, N_EXPERTS), 0.01),
        "w_gu": nrm((L, N_EXPERTS, D_MODEL, 2 * D_FF), D_MODEL ** -0.5),
        "b_gu": nrm((L, N_EXPERTS, 2 * D_FF), 0.02),
        "w_down": nrm((L, N_EXPERTS, D_FF, D_MODEL), D_FF ** -0.5),
        "b_down": nrm((L, N_EXPERTS, D_MODEL), 0.02),
    }


def reference(x, mem, norm_mix_w, norm_mem_w, w_in, w_gate, b_gate, pool_w, pool_scale,
              q_norm_w, k_norm_w, lambda_q1, lambda_k1, lambda_q2, lambda_k2, diff_norm_w,
              rel_bias, w_mem_kv, mem_q_norm_w, mem_k_norm_w, w_up_pool, w_up_diff, w_up_mem,
              w_out, norm_ffn_w, router_w, router_b, w_gu, b_gu, w_down, b_down):
    B, S, D = x.shape
    M = mem.shape[1]
    for l in range(DEPTH):
        h = rms_norm(x, norm_mix_w[l])
        proj = h @ w_in[l]
        u_pool, q, k, v, q_mem = jnp.split(proj, IN_SPLITS, axis=-1)

        pool_out = pool_mixer(u_pool, pool_w[l], pool_scale[l])

        q = rms_norm(q.reshape(B, S, DIFF_HEADS, 2, DIFF_QK_DIM), q_norm_w[l]).transpose(0, 2, 3, 1, 4)
        k = rms_norm(k.reshape(B, S, DIFF_HEADS, 2, DIFF_QK_DIM), k_norm_w[l]).transpose(0, 2, 3, 1, 4)
        v = v.reshape(B, S, DIFF_HEADS, DIFF_V_DIM).transpose(0, 2, 1, 3)
        lam_init = 0.8 - 0.6 * math.exp(-0.3 * l)
        lam = (jnp.exp(jnp.sum(lambda_q1[l].astype(jnp.float32) * lambda_k1[l].astype(jnp.float32)))
               - jnp.exp(jnp.sum(lambda_q2[l].astype(jnp.float32) * lambda_k2[l].astype(jnp.float32)))
               + lam_init)
        o = diff_attention(q, k, v, rel_bias, lam)
        o = rms_norm(o, diff_norm_w[l]) * (1.0 - lam_init)
        diff_out = o.transpose(0, 2, 1, 3).reshape(B, S, DIFF_WIDTH)

        mem_h = rms_norm(mem, norm_mem_w[l])
        mk, mv = jnp.split(mem_h @ w_mem_kv[l], 2, axis=-1)
        mq = rms_norm(q_mem.reshape(B, S, MEM_HEADS, MEM_HEAD_DIM), mem_q_norm_w[l])
        mk = rms_norm(mk.reshape(B, M, MEM_HEADS, MEM_HEAD_DIM), mem_k_norm_w[l])
        mv = mv.reshape(B, M, MEM_HEADS, MEM_HEAD_DIM)
        ml = jnp.einsum('bshd,bmhd->bhsm', mq, mk).astype(jnp.float32) / math.sqrt(MEM_HEAD_DIM)
        mp = jax.nn.softmax(ml, axis=-1).astype(mv.dtype)
        mem_out = jnp.einsum('bhsm,bmhd->bshd', mp, mv).reshape(B, S, MEM_WIDTH)

        g = jax.nn.sigmoid((h @ w_gate[l] + b_gate[l]).astype(jnp.float32)).astype(x.dtype)
        g = g.reshape(B, S, N_BRANCHES, D)
        y = (g[:, :, 0] * (pool_out @ w_up_pool[l])
             + g[:, :, 1] * (diff_out @ w_up_diff[l])
             + g[:, :, 2] * (mem_out @ w_up_mem[l]))
        x = x + y @ w_out[l]

        h2 = rms_norm(x, norm_ffn_w[l]).reshape(B * S, D)
        ff = moe_ffn(h2, router_w[l], router_b[l], w_gu[l], b_gu[l], w_down[l], b_down[l])
        x = x + ff.reshape(B, S, D)
    return x
```

```python
import functools
import math

import numpy as np
import jax
import jax.numpy as jnp
from jax import lax
from jax.experimental import pallas as pl
from jax.experimental.pallas import tpu as pltpu

F32 = jnp.float32
BF16 = jnp.bfloat16

EPS = 1e-6
DIFF_HEADS = 8
DIFF_QK_DIM = 64
DIFF_V_DIM = 128
MEM_HEADS = 4
MEM_HEAD_DIM = 128
POOL_WINDOWS = (2, 4, 8, 16)
POOL_GROUP_DIM = 128
NUM_BUCKETS = 32
MAX_DISTANCE = 128
TOP_K = 4
SWIGLU_LIMIT = 7.0
SWIGLU_ALPHA = 1.702
LAM_INIT = 0.8 - 0.6 * math.exp(-0.3 * 0)

LANES = 128
VMEM_LIMIT = 56 * 1024 * 1024

ATT_BLOCK = 256
NT_DIMS = (((1,), (1,)), ((), ()))


def _cparams(*sem):
    return pltpu.CompilerParams(dimension_semantics=sem, vmem_limit_bytes=VMEM_LIMIT)


def _rms_rows(x, w):
    ms = jnp.mean(x * x, axis=-1, keepdims=True)
    return x * lax.rsqrt(ms + EPS) * w


def _proj_kernel(x_ref, nw_ref, w_ref, g_ref, qw_ref, kw_ref, mw_ref, o_ref, h_ref, *, chunk, segs):
    h_ref[...] = _rms_rows(x_ref[...], nw_ref[...]).astype(BF16)
    n = w_ref.shape[1]
    half = chunk // 2
    for c in range(n // chunk):
        c0 = c * chunk
        y = jnp.dot(h_ref[...], w_ref[:, c0:c0 + chunk], preferred_element_type=F32)
        kind = segs[c]
        if kind in ("q", "k"):
            wn = (qw_ref if kind == "q" else kw_ref)[...]
            outs = []
            for s in range(2):
                ys = y[:, s * half:(s + 1) * half]
                ss = jnp.dot((ys * ys).astype(BF16), g_ref[...], preferred_element_type=F32)
                outs.append(ys * lax.rsqrt(ss * (1.0 / DIFF_QK_DIM) + EPS) * wn[:, s * half:(s + 1) * half])
            y = jnp.concatenate(outs, axis=-1)
        elif kind == "m":
            outs = []
            for s in range(chunk // MEM_HEAD_DIM):
                ys = y[:, s * MEM_HEAD_DIM:(s + 1) * MEM_HEAD_DIM]
                outs.append(_rms_rows(ys, mw_ref[...]))
            y = jnp.concatenate(outs, axis=-1)
        o_ref[:, c0:c0 + chunk] = y.astype(BF16)


def _proj(x2, norm_w, w_in, gmat, qw, kw, mw, *, tm, segs, chunk):
    t, d = x2.shape
    n = w_in.shape[1]
    full = lambda shape: pl.BlockSpec(shape, lambda i: (0,) * len(shape))
    return pl.pallas_call(
        functools.partial(_proj_kernel, chunk=chunk, segs=segs),
        out_shape=jax.ShapeDtypeStruct((t, n), BF16),
        grid=(t // tm,),
        in_specs=[
            pl.BlockSpec((tm, d), lambda i: (i, 0)),
            full((1, d)),
            pl.BlockSpec((d, n), lambda i: (0, 0), pipeline_mode=pl.Buffered(1)),
            full(gmat.shape),
            full(qw.shape),
            full(kw.shape),
            full(mw.shape),
        ],
        out_specs=pl.BlockSpec((tm, n), lambda i: (i, 0)),
        scratch_shapes=[pltpu.VMEM((tm, d), BF16)],
        compiler_params=_cparams("parallel"),
        name="proj",
    )(x2, norm_w, w_in, gmat, qw, kw, mw)


def _pool_kernel(u_ref, pw_ref, ps_ref, o_ref):
    s_len = u_ref.shape[0]
    row = lax.broadcasted_iota(jnp.int32, (s_len, POOL_GROUP_DIM), 0)
    for g, w in enumerate(POOL_WINDOWS):
        cols = slice(g * POOL_GROUP_DIM, (g + 1) * POOL_GROUP_DIM)
        ug = u_ref[:, cols].astype(F32)
        acc = ug
        span = 1
        while span < w:
            shifted = pltpu.roll(acc, span, 0)
            acc = acc + jnp.where(row >= span, shifted, 0.0)
            span *= 2
        cnt = jnp.minimum(row + 1, w).astype(F32)
        pooled = acc / cnt - ug
        mixed = jnp.dot(pooled.astype(BF16), pw_ref[g], preferred_element_type=F32)
        o_ref[:, cols] = (mixed * ps_ref[:, cols]).astype(BF16)


def _pool(proj, pool_w, pool_scale, *, batch, seq):
    width = len(POOL_WINDOWS) * POOL_GROUP_DIM
    return pl.pallas_call(
        _pool_kernel,
        out_shape=jax.ShapeDtypeStruct((batch * seq, width), BF16),
        grid=(batch,),
        in_specs=[
            pl.BlockSpec((seq, width), lambda b: (b, 0)),
            pl.BlockSpec(pool_w.shape, lambda b: (0, 0, 0)),
            pl.BlockSpec((1, width), lambda b: (0, 0)),
        ],
        out_specs=pl.BlockSpec((seq, width), lambda b: (b, 0)),
        compiler_params=_cparams("parallel"),
        name="pool",
    )(proj, pool_w, pool_scale)


def _bucket_tables(blk):
    exact = NUM_BUCKETS // 2
    qi = np.arange(blk)[:, None]
    kj = np.arange(blk)[None, :]
    out = []
    for off in (0, 1):
        rel = qi - kj + off * blk
        n = np.maximum(rel, 0)
        nf = np.maximum(n, 1).astype(np.float64)
        large = exact + (np.log(nf / exact) / math.log(MAX_DISTANCE / exact) * (NUM_BUCKETS - exact)).astype(np.int64)
        large = np.minimum(large, NUM_BUCKETS - 1)
        bucket = np.where(n < exact, n, large)
        out.append(np.where(rel >= 0, bucket, -1))
    return np.stack(out).astype(np.int32)


def _bias_kernel(rb_ref, bucket_ref, o_ref):
    h = pl.program_id(0)
    for t in range(2):
        bk = bucket_ref[t]
        acc = jnp.full(bk.shape, -jnp.inf, F32)
        for b in range(NUM_BUCKETS):
            acc = jnp.where(bk == b, rb_ref[b * DIFF_HEADS + h], acc)
        o_ref[0, t] = acc


def _bias_tables(rel_bias_flat, buckets):
    _, blk, _ = buckets.shape
    return pl.pallas_call(
        _bias_kernel,
        out_shape=jax.ShapeDtypeStruct((DIFF_HEADS, 2, blk, blk), F32),
        grid=(DIFF_HEADS,),
        in_specs=[
            pl.BlockSpec(memory_space=pltpu.SMEM),
            pl.BlockSpec(buckets.shape, lambda h: (0, 0, 0)),
        ],
        out_specs=pl.BlockSpec((1, 2, blk, blk), lambda h: (h, 0, 0, 0)),
        compiler_params=_cparams("arbitrary"),
        name="bias_tables",
    )(rel_bias_flat, buckets)


def _attn_kernel(rb_ref, q_ref, k_ref, v_ref, tbl_ref, lam_ref, nw_ref, o_ref, *, blk):
    h = pl.program_id(1)
    i = pl.program_id(2)
    q = q_ref[...]
    lane = lax.broadcasted_iota(jnp.int32, q.shape, 1)
    zero = jnp.zeros_like(q)
    q1 = jnp.where(lane < DIFF_QK_DIM, q, zero)
    q2 = jnp.where(lane >= DIFF_QK_DIM, q, zero)
    far = rb_ref[(NUM_BUCKETS - 1) * DIFF_HEADS + h]

    def one_map(qm, kj, vj, bias, m, l, acc):
        s = lax.dot_general(qm, kj, NT_DIMS, preferred_element_type=F32) + bias
        m_new = jnp.maximum(m, jnp.max(s, axis=-1, keepdims=True))
        p = jnp.exp(s - m_new)
        alpha = jnp.exp(m - m_new)
        l_new = alpha * l + jnp.sum(p, axis=-1, keepdims=True)
        acc_new = alpha * acc + jnp.dot(p.astype(BF16), vj, preferred_element_type=F32)
        return m_new, l_new, acc_new

    def body(j, carry):
        m1, l1, a1, m2, l2, a2 = carry
        off = pl.multiple_of(j * blk, blk)
        kj = k_ref[pl.ds(off, blk), :]
        vj = v_ref[pl.ds(off, blk), :]
        bias = jnp.where(j == i, tbl_ref[0, 0], jnp.where(j == i - 1, tbl_ref[0, 1], far))
        m1, l1, a1 = one_map(q1, kj, vj, bias, m1, l1, a1)
        m2, l2, a2 = one_map(q2, kj, vj, bias, m2, l2, a2)
        return m1, l1, a1, m2, l2, a2

    m0 = jnp.full((blk, 1), -jnp.inf, F32)
    l0 = jnp.zeros((blk, 1), F32)
    a0 = jnp.zeros((blk, DIFF_V_DIM), F32)
    m1, l1, a1, m2, l2, a2 = lax.fori_loop(0, i + 1, body, (m0, l0, a0, m0, l0, a0))

    lv = lam_ref[...]
    lam = (jnp.exp(jnp.sum(lv[0:1] * lv[1:2], axis=-1, keepdims=True))
           - jnp.exp(jnp.sum(lv[2:3] * lv[3:4], axis=-1, keepdims=True)) + LAM_INIT)
    o = a1 / l1 - lam * (a2 / l2)
    o_ref[...] = (_rms_rows(o, nw_ref[...]) * (1.0 - LAM_INIT)).astype(BF16)


def _diff_attention(proj, rel_bias_flat, tables, lam_vecs, diff_norm_w, *, batch, seq, q_col, k_col, v_col):
    blk = ATT_BLOCK
    nq = seq // blk
    qcb, kcb, vcb = q_col // LANES, k_col // LANES, v_col // LANES
    return pl.pallas_call(
        functools.partial(_attn_kernel, blk=blk),
        out_shape=jax.ShapeDtypeStruct((batch * seq, DIFF_HEADS * DIFF_V_DIM), BF16),
        grid=(batch, DIFF_HEADS, nq),
        in_specs=[
            pl.BlockSpec(memory_space=pltpu.SMEM),
            pl.BlockSpec((blk, LANES), lambda b, h, i: (b * nq + i, qcb + h)),
            pl.BlockSpec((seq, LANES), lambda b, h, i: (b, kcb + h)),
            pl.BlockSpec((seq, LANES), lambda b, h, i: (b, vcb + h)),
            pl.BlockSpec((1, 2, blk, blk), lambda b, h, i: (h, 0, 0, 0)),
            pl.BlockSpec(lam_vecs.shape, lambda b, h, i: (0, 0)),
            pl.BlockSpec((1, DIFF_V_DIM), lambda b, h, i: (0, 0)),
        ],
        out_specs=pl.BlockSpec((blk, DIFF_V_DIM), lambda b, h, i: (b * nq + i, h)),
        compiler_params=_cparams("parallel", "parallel", "arbitrary"),
        name="diff_attn",
    )(rel_bias_flat, proj, proj, proj, tables, lam_vecs, diff_norm_w)


def _memkv_kernel(m_ref, nw_ref, w_ref, kw_ref, k_ref, v_ref):
    hm = _rms_rows(m_ref[...], nw_ref[...]).astype(BF16)
    kv = jnp.dot(hm, w_ref[...], preferred_element_type=F32)
    width = MEM_HEADS * MEM_HEAD_DIM
    for g in range(MEM_HEADS):
        cols = slice(g * MEM_HEAD_DIM, (g + 1) * MEM_HEAD_DIM)
        k_ref[:, cols] = _rms_rows(kv[:, cols], kw_ref[...]).astype(BF16)
    v_ref[...] = kv[:, width:].astype(BF16)


def _mem_kv(mem2, norm_w, w_kv, k_norm_w, *, batch, mlen):
    d = mem2.shape[1]
    width = MEM_HEADS * MEM_HEAD_DIM
    out = jax.ShapeDtypeStruct((batch * mlen, width), BF16)
    return pl.pallas_call(
        _memkv_kernel,
        out_shape=(out, out),
        grid=(batch,),
        in_specs=[
            pl.BlockSpec((mlen, d), lambda b: (b, 0)),
            pl.BlockSpec((1, d), lambda b: (0, 0)),
            pl.BlockSpec(w_kv.shape, lambda b: (0, 0)),
            pl.BlockSpec((1, MEM_HEAD_DIM), lambda b: (0, 0)),
        ],
        out_specs=(pl.BlockSpec((mlen, width), lambda b: (b, 0)),
                   pl.BlockSpec((mlen, width), lambda b: (b, 0))),
        compiler_params=_cparams("parallel"),
        name="mem_kv",
    )(mem2, norm_w, w_kv, k_norm_w)


def _memattn_kernel(q_ref, k_ref, v_ref, o_ref):
    for g in range(MEM_HEADS):
        cols = slice(g * MEM_HEAD_DIM, (g + 1) * MEM_HEAD_DIM)
        s = lax.dot_general(q_ref[:, cols], k_ref[:, cols], NT_DIMS, preferred_element_type=F32)
        s = s - jnp.max(s, axis=-1, keepdims=True)
        p = jnp.exp(s)
        p = p / jnp.sum(p, axis=-1, keepdims=True)
        o_ref[:, cols] = jnp.dot(p.astype(BF16), v_ref[:, cols], preferred_element_type=F32).astype(BF16)


def _mem_attention(proj, mk, mv, *, batch, seq, mlen, q_col, tm):
    width = MEM_HEADS * MEM_HEAD_DIM
    nt = seq // tm
    qcb = q_col // width
    return pl.pallas_call(
        _memattn_kernel,
        out_shape=jax.ShapeDtypeStruct((batch * seq, width), BF16),
        grid=(batch, nt),
        in_specs=[
            pl.BlockSpec((tm, width), lambda b, i: (b * nt + i, qcb)),
            pl.BlockSpec((mlen, width), lambda b, i: (b, 0)),
            pl.BlockSpec((mlen, width), lambda b, i: (b, 0)),
        ],
        out_specs=pl.BlockSpec((tm, width), lambda b, i: (b * nt + i, 0)),
        compiler_params=_cparams("parallel", "parallel"),
        name="mem_attn",
    )(proj, mk, mv)


def _merge_kernel(x_ref, nw_ref, pool_ref, diff_ref, mem_ref,
                  wg0_ref, wg1_ref, wg2_ref, bg0_ref, bg1_ref, bg2_ref,
                  wup_ref, wud_ref, wum_ref, wo_ref, o_ref, h_ref, acc_ref):
    n = pl.program_id(1)

    @pl.when(n == 0)
    def _():
        x = x_ref[...]
        h_ref[...] = _rms_rows(x, nw_ref[...]).astype(BF16)
        acc_ref[...] = x

    h = h_ref[...]

    def branch(a_ref, wg_ref, bg_ref, wu_ref):
        gate = jax.nn.sigmoid(jnp.dot(h, wg_ref[...], preferred_element_type=F32) + bg_ref[...])
        return gate * jnp.dot(a_ref[...], wu_ref[...], preferred_element_type=F32)

    y = (branch(pool_ref, wg0_ref, bg0_ref, wup_ref)
         + branch(diff_ref, wg1_ref, bg1_ref, wud_ref)
         + branch(mem_ref, wg2_ref, bg2_ref, wum_ref))
    acc_ref[...] += jnp.dot(y.astype(BF16), wo_ref[...], preferred_element_type=F32)

    @pl.when(n == pl.num_programs(1) - 1)
    def _():
        o_ref[...] = acc_ref[...]


def _merge(x2, norm_w, pool_out, diff_out, mem_out, w_gate, b_gate, w_up_pool, w_up_diff, w_up_mem, w_out,
           *, tm, tn):
    t, d = x2.shape
    nn = d // tn
    rows = lambda width: pl.BlockSpec((tm, width), lambda i, n: (i, 0))
    gate_w = lambda br: pl.BlockSpec((d, tn), lambda i, n: (0, br * nn + n))
    gate_b = lambda br: pl.BlockSpec((1, tn), lambda i, n: (0, br * nn + n))
    up_w = lambda width: pl.BlockSpec((width, tn), lambda i, n: (0, n))
    return pl.pallas_call(
        _merge_kernel,
        out_shape=jax.ShapeDtypeStruct((t, d), F32),
        grid=(t // tm, nn),
        in_specs=[
            rows(d),
            pl.BlockSpec((1, d), lambda i, n: (0, 0)),
            rows(pool_out.shape[1]), rows(diff_out.shape[1]), rows(mem_out.shape[1]),
            gate_w(0), gate_w(1), gate_w(2),
            gate_b(0), gate_b(1), gate_b(2),
            up_w(pool_out.shape[1]), up_w(diff_out.shape[1]), up_w(mem_out.shape[1]),
            pl.BlockSpec((tn, d), lambda i, n: (n, 0)),
        ],
        out_specs=pl.BlockSpec((tm, d), lambda i, n: (i, 0)),
        scratch_shapes=[pltpu.VMEM((tm, d), BF16), pltpu.VMEM((tm, d), F32)],
        compiler_params=_cparams("parallel", "arbitrary"),
        name="merge",
    )(x2, norm_w, pool_out, diff_out, mem_out, w_gate, w_gate, w_gate, b_gate, b_gate, b_gate,
      w_up_pool, w_up_diff, w_up_mem, w_out)


def _split_bf16(a):
    hi = a.astype(BF16)
    lo = (a - hi.astype(F32)).astype(BF16)
    return hi, lo


def _router_kernel(x_ref, nw_ref, whi_ref, wlo_ref, b_ref, idx_ref, gate_ref):
    h = _rms_rows(x_ref[...], nw_ref[...])
    hi, lo = _split_bf16(h)
    logits = (jnp.dot(hi, whi_ref[...], preferred_element_type=F32)
              + jnp.dot(hi, wlo_ref[...], preferred_element_type=F32)
              + jnp.dot(lo, whi_ref[...], preferred_element_type=F32)) + b_ref[...]
    tm, ne = logits.shape
    eid = lax.broadcasted_iota(jnp.int32, (tm, ne), 1)
    vals, ids = [], []
    cur = logits
    for _ in range(TOP_K):
        mx = jnp.max(cur, axis=-1, keepdims=True)
        sel = jnp.min(jnp.where(cur == mx, eid, ne), axis=-1, keepdims=True)
        vals.append(mx)
        ids.append(sel)
        cur = jnp.where(eid == sel, -jnp.inf, cur)
    exps = [jnp.exp(v - vals[0]) for v in vals]
    denom = exps[0] + exps[1] + exps[2] + exps[3]
    lane = lax.broadcasted_iota(jnp.int32, (tm, LANES), 1)
    idx_out = jnp.zeros((tm, LANES), jnp.int32)
    gate_out = jnp.zeros((tm, LANES), F32)
    for k in range(TOP_K):
        idx_out = jnp.where(lane == k, ids[k], idx_out)
        gate_out = jnp.where(lane == k, exps[k] / denom, gate_out)
    idx_ref[...] = idx_out
    gate_ref[...] = gate_out


def _router(x1, norm_w, w_hi, w_lo, bias, *, tm):
    t, d = x1.shape
    ne = w_hi.shape[1]
    return pl.pallas_call(
        _router_kernel,
        out_shape=(jax.ShapeDtypeStruct((t, LANES), jnp.int32), jax.ShapeDtypeStruct((t, LANES), F32)),
        grid=(t // tm,),
        in_specs=[
            pl.BlockSpec((tm, d), lambda i: (i, 0)),
            pl.BlockSpec((1, d), lambda i: (0, 0)),
            pl.BlockSpec((d, ne), lambda i: (0, 0)),
            pl.BlockSpec((d, ne), lambda i: (0, 0)),
            pl.BlockSpec((1, ne), lambda i: (0, 0)),
        ],
        out_specs=(pl.BlockSpec((tm, LANES), lambda i: (i, 0)),
                   pl.BlockSpec((tm, LANES), lambda i: (i, 0))),
        compiler_params=_cparams("parallel"),
        name="router",
    )(x1, norm_w, w_hi, w_lo, bias)


def _moe_kernel(be_ref, na_ref, tok_ref, tokn_ref, dst_ref, x_hbm, nw_ref,
                wg_ref, wu_ref, bg_ref, bu_ref, wd_ref, bd_ref, y_hbm,
                xbuf, xn_ref, acc_ref, ybuf, gsem, ssem, *, tm, unroll):
    b = pl.program_id(0)
    f = pl.program_id(1)
    nb = pl.num_programs(0)
    nf = pl.num_programs(1)
    nact = na_ref[0]
    slot = lax.rem(b, 2)

    def gather_start(idx_ref, s):
        def one(r, c):
            t = idx_ref[0, 0, r]
            pltpu.make_async_copy(x_hbm.at[pl.ds(t, 1)], xbuf.at[s, pl.ds(r, 1)], gsem.at[s]).start()
            return c
        lax.fori_loop(0, tm, one, 0, unroll=unroll)

    def gather_wait(s):
        pltpu.make_async_copy(x_hbm.at[pl.ds(0, tm)], xbuf.at[s], gsem.at[s]).wait()

    def scatter_start(s):
        def one(r, c):
            t = dst_ref[0, 0, r]
            pltpu.make_async_copy(ybuf.at[s, pl.ds(r, 1)], y_hbm.at[pl.ds(t, 1)], ssem.at[s]).start()
            return c
        lax.fori_loop(0, tm, one, 0, unroll=unroll)

    def scatter_wait(s):
        pltpu.make_async_copy(ybuf.at[s], y_hbm.at[pl.ds(0, tm)], ssem.at[s]).wait()

    active = b < nact

    @pl.when((b == 0) & (f == 0))
    def _():
        gather_start(tok_ref, 0)

    @pl.when(active & (f == 0))
    def _():
        gather_wait(slot)

        @pl.when(b + 1 < nact)
        def _():
            gather_start(tokn_ref, 1 - slot)

        xn_ref[...] = _rms_rows(xbuf[slot], nw_ref[...]).astype(BF16)

    @pl.when(active)
    def _():
        xn = xn_ref[...]
        gate = jnp.dot(xn, wg_ref[0], preferred_element_type=F32) + bg_ref[0]
        up = jnp.dot(xn, wu_ref[0], preferred_element_type=F32) + bu_ref[0]
        gate = jnp.minimum(gate, SWIGLU_LIMIT)
        up = jnp.clip(up, -SWIGLU_LIMIT, SWIGLU_LIMIT)
        act = gate * jax.nn.sigmoid(gate * SWIGLU_ALPHA) * (up + 1.0)
        part = jnp.dot(act.astype(BF16), wd_ref[0], preferred_element_type=F32)

        @pl.when(f == 0)
        def _():
            acc_ref[...] = part

        @pl.when(f > 0)
        def _():
            acc_ref[...] += part

    @pl.when(active & (f == nf - 1))
    def _():
        @pl.when(b >= 2)
        def _():
            scatter_wait(slot)

        ybuf[slot] = acc_ref[...] + bd_ref[0]
        scatter_start(slot)

    @pl.when((b == nb - 1) & (f == nf - 1))
    def _():
        scatter_wait(lax.rem(nact - 1, 2))

        @pl.when(nact >= 2)
        def _():
            scatter_wait(lax.rem(nact, 2))


def _moe(blk_expert, nact, src_tok, dst_row, x1, norm_w, w_gu, b_gu, w_down, b_down, *, tm, tf, n_out_rows):
    t, d = x1.shape
    ne, _, two_f = w_gu.shape
    nf = two_f // 2 // tf
    nblk = src_tok.shape[0]

    def eidx(b, be, na):
        return be[jnp.minimum(b, na[0] - 1)]

    def fidx(b, f, na):
        return jnp.where(b < na[0], f, nf - 1)

    idx_blk = lambda shift: pl.BlockSpec(
        (1, 1, tm), lambda b, f, be, na: (jnp.minimum(b + shift, nblk - 1), 0, 0), memory_space=pltpu.SMEM)
    grid_spec = pltpu.PrefetchScalarGridSpec(
        num_scalar_prefetch=2,
        grid=(nblk, nf),
        in_specs=[
            idx_blk(0), idx_blk(1), idx_blk(0),
            pl.BlockSpec(memory_space=pl.ANY),
            pl.BlockSpec((1, d), lambda b, f, be, na: (0, 0)),
            pl.BlockSpec((1, d, tf), lambda b, f, be, na: (eidx(b, be, na), 0, fidx(b, f, na))),
            pl.BlockSpec((1, d, tf), lambda b, f, be, na: (eidx(b, be, na), 0, nf + fidx(b, f, na))),
            pl.BlockSpec((1, 1, tf), lambda b, f, be, na: (eidx(b, be, na), 0, fidx(b, f, na))),
            pl.BlockSpec((1, 1, tf), lambda b, f, be, na: (eidx(b, be, na), 0, nf + fidx(b, f, na))),
            pl.BlockSpec((1, tf, d), lambda b, f, be, na: (eidx(b, be, na), fidx(b, f, na), 0)),
            pl.BlockSpec((1, 1, d), lambda b, f, be, na: (eidx(b, be, na), 0, 0)),
        ],
        out_specs=pl.BlockSpec(memory_space=pl.ANY),
        scratch_shapes=[
            pltpu.VMEM((2, tm, d), F32),
            pltpu.VMEM((tm, d), BF16),
            pltpu.VMEM((tm, d), F32),
            pltpu.VMEM((2, tm, d), F32),
            pltpu.SemaphoreType.DMA((2,)),
            pltpu.SemaphoreType.DMA((2,)),
        ],
    )
    return pl.pallas_call(
        functools.partial(_moe_kernel, tm=tm, unroll=8),
        out_shape=jax.ShapeDtypeStruct((n_out_rows, d), F32),
        grid_spec=grid_spec,
        compiler_params=_cparams("arbitrary", "arbitrary"),
        name="moe",
    )(blk_expert, nact, src_tok, src_tok, dst_row, x1, norm_w, w_gu, w_gu, b_gu, b_gu, w_down, b_down)


def _combine_kernel(x_ref, g_ref, y0_ref, y1_ref, y2_ref, y3_ref, o_ref):
    g = g_ref[...]
    out = x_ref[...]
    for k, y_ref in enumerate((y0_ref, y1_ref, y2_ref, y3_ref)):
        out = out + g[:, k:k + 1] * y_ref[...]
    o_ref[...] = out


def _combine(x1, gates, yk, *, tc):
    t, d = x1.shape
    nt = t // tc
    ysp = lambda k: pl.BlockSpec((tc, d), lambda i: (k * nt + i, 0))
    return pl.pallas_call(
        _combine_kernel,
        out_shape=jax.ShapeDtypeStruct((t, d), F32),
        grid=(nt,),
        in_specs=[
            pl.BlockSpec((tc, d), lambda i: (i, 0)),
            pl.BlockSpec((tc, LANES), lambda i: (i, 0)),
            ysp(0), ysp(1), ysp(2), ysp(3),
        ],
        out_specs=pl.BlockSpec((tc, d), lambda i: (i, 0)),
        compiler_params=_cparams("parallel"),
        name="combine",
    )(x1, gates, yk, yk, yk, yk)


def _routing_plan(top_idx, n_experts, tm):
    t = top_idx.shape[0]
    tk = t * TOP_K
    nblk = tk // tm + n_experts
    p = nblk * tm
    flat_e = top_idx.reshape(tk)
    order = jnp.argsort(flat_e).astype(jnp.int32)
    e_sorted = flat_e[order]
    counts = jnp.zeros((n_experts,), jnp.int32).at[flat_e].add(1)
    padded = ((counts + tm - 1) // tm) * tm
    start = jnp.cumsum(counts) - counts
    pcum = jnp.cumsum(padded)
    pstart = pcum - padded
    dest = pstart[e_sorted] + (jnp.arange(tk, dtype=jnp.int32) - start[e_sorted])
    row_flat = jnp.full((p,), -1, jnp.int32).at[dest].set(order)
    valid = row_flat >= 0
    tok = row_flat // TOP_K
    kk = row_flat - tok * TOP_K
    src_tok = jnp.where(valid, tok, 0)
    pad_ord = jnp.cumsum(jnp.logical_not(valid).astype(jnp.int32)) - 1
    dst_row = jnp.where(valid, kk * t + tok, tk + pad_ord)
    blk_start = jnp.arange(nblk, dtype=jnp.int32) * tm
    blk_expert = jnp.minimum(jnp.searchsorted(pcum, blk_start, side="right"), n_experts - 1).astype(jnp.int32)
    nact = (pcum[-1] // tm).astype(jnp.int32).reshape(1)
    return (blk_expert, nact, src_tok.reshape(nblk, 1, tm), dst_row.reshape(nblk, 1, tm), tk + n_experts * tm)


def kernel(x, mem, norm_mix_w, norm_mem_w, w_in, w_gate, b_gate, pool_w, pool_scale, q_norm_w, k_norm_w,
           lambda_q1, lambda_k1, lambda_q2, lambda_k2, diff_norm_w, rel_bias, w_mem_kv, mem_q_norm_w,
           mem_k_norm_w, w_up_pool, w_up_diff, w_up_mem, w_out, norm_ffn_w, router_w, router_b, w_gu, b_gu,
           w_down, b_down):
    batch, seq, d = x.shape
    mlen = mem.shape[1]
    t = batch * seq
    assert norm_mix_w.shape[0] == 1, "single-layer block"
    n_experts = router_w.shape[2]

    pool_width = len(POOL_WINDOWS) * POOL_GROUP_DIM
    qk_width = DIFF_HEADS * 2 * DIFF_QK_DIM
    v_width = DIFF_HEADS * DIFF_V_DIM
    mem_width = MEM_HEADS * MEM_HEAD_DIM
    q_col = pool_width
    k_col = q_col + qk_width
    v_col = k_col + qk_width
    mq_col = v_col + v_width
    chunk = 512
    assert w_in.shape[2] == mq_col + mem_width
    segs = (("p",) * (pool_width // chunk) + ("q",) * (qk_width // chunk) + ("k",) * (qk_width // chunk)
            + ("v",) * (v_width // chunk) + ("m",) * (mem_width // chunk))

    x2 = x.reshape(t, d)
    row = lambda a: a.reshape(1, -1)

    half = chunk // 2
    gmat = jnp.asarray(np.kron(np.eye(half // DIFF_QK_DIM), np.ones((DIFF_QK_DIM, DIFF_QK_DIM))), BF16)
    reps = chunk // DIFF_QK_DIM
    qw = row(jnp.tile(q_norm_w[0] * (1.0 / math.sqrt(DIFF_QK_DIM)), reps))
    kw = row(jnp.tile(k_norm_w[0], reps))
    mw = row(mem_q_norm_w[0] * (1.0 / math.sqrt(MEM_HEAD_DIM)))
    proj = _proj(x2, row(norm_mix_w[0]), w_in[0].astype(BF16), gmat, qw, kw, mw, tm=512, segs=segs, chunk=chunk)

    pool_out = _pool(proj, pool_w[0].astype(BF16), row(pool_scale[0]), batch=batch, seq=seq)

    rel_flat = rel_bias.reshape(-1)
    tables = _bias_tables(rel_flat, jnp.asarray(_bucket_tables(ATT_BLOCK)))
    lam_vecs = jnp.concatenate([lambda_q1, lambda_k1, lambda_q2, lambda_k2], axis=0)
    diff_out = _diff_attention(proj, rel_flat, tables, lam_vecs, row(diff_norm_w[0]),
                               batch=batch, seq=seq, q_col=q_col, k_col=k_col, v_col=v_col)

    mk, mv = _mem_kv(mem.reshape(batch * mlen, d), row(norm_mem_w[0]), w_mem_kv[0].astype(BF16),
                     row(mem_k_norm_w[0]), batch=batch, mlen=mlen)
    mem_out = _mem_attention(proj, mk, mv, batch=batch, seq=seq, mlen=mlen, q_col=mq_col, tm=512)

    x1 = _merge(x2, row(norm_mix_w[0]), pool_out, diff_out, mem_out, w_gate[0].astype(BF16), row(b_gate[0]),
                w_up_pool[0].astype(BF16), w_up_diff[0].astype(BF16), w_up_mem[0].astype(BF16),
                w_out[0].astype(BF16), tm=512, tn=512)

    rw = router_w[0]
    rw_hi = rw.astype(BF16)
    rw_lo = (rw - rw_hi.astype(F32)).astype(BF16)
    idx_pad, gate_pad = _router(x1, row(norm_ffn_w[0]), rw_hi, rw_lo, row(router_b[0]), tm=512)
    tm_moe = 512
    blk_expert, nact, src_tok, dst_row, n_out_rows = _routing_plan(idx_pad[:, :TOP_K], n_experts, tm_moe)
    yk = _moe(blk_expert, nact, src_tok, dst_row, x1, row(norm_ffn_w[0]),
              w_gu[0].astype(BF16), b_gu[0].reshape(n_experts, 1, -1),
              w_down[0].astype(BF16), b_down[0].reshape(n_experts, 1, -1),
              tm=tm_moe, tf=512, n_out_rows=n_out_rows)
    out = _combine(x1, gate_pad, yk, tc=256)
    return out.reshape(batch, seq, d)
```

```python
import functools
import math

import numpy as np
import jax
import jax.numpy as jnp
from jax import lax
from jax.experimental import pallas as pl
from jax.experimental.pallas import tpu as pltpu

F32 = jnp.float32
BF16 = jnp.bfloat16

EPS = 1e-6
DIFF_HEADS = 8
DIFF_QK_DIM = 64
DIFF_V_DIM = 128
MEM_HEADS = 4
MEM_HEAD_DIM = 128
POOL_WINDOWS = (2, 4, 8, 16)
POOL_GROUP_DIM = 128
NUM_BUCKETS = 32
MAX_DISTANCE = 128
TOP_K = 4
SWIGLU_LIMIT = 7.0
SWIGLU_ALPHA = 1.702
LAM_INIT = 0.8 - 0.6 * math.exp(-0.3 * 0)

LANES = 128
VMEM_LIMIT = 56 * 1024 * 1024

ATT_BLOCK = 256
NT_DIMS = (((1,), (1,)), ((), ()))


def _cparams(*sem):
    return pltpu.CompilerParams(dimension_semantics=sem, vmem_limit_bytes=VMEM_LIMIT)


def _rms_rows(x, w):
    ms = jnp.mean(x * x, axis=-1, keepdims=True)
    return x * lax.rsqrt(ms + EPS) * w


def _proj_kernel(x_ref, nw_ref, w_ref, g_ref, qw_ref, kw_ref, mw_ref, o_ref, h_ref, *, chunk, segs):
    h_ref[...] = _rms_rows(x_ref[...], nw_ref[...]).astype(BF16)
    n = w_ref.shape[1]
    half = chunk // 2
    for c in range(n // chunk):
        c0 = c * chunk
        y = jnp.dot(h_ref[...], w_ref[:, c0:c0 + chunk], preferred_element_type=F32)
        kind = segs[c]
        if kind in ("q", "k"):
            wn = (qw_ref if kind == "q" else kw_ref)[...]
            outs = []
            for s in range(2):
                ys = y[:, s * half:(s + 1) * half]
                ss = jnp.dot((ys * ys).astype(BF16), g_ref[...], preferred_element_type=F32)
                outs.append(ys * lax.rsqrt(ss * (1.0 / DIFF_QK_DIM) + EPS) * wn[:, s * half:(s + 1) * half])
            y = jnp.concatenate(outs, axis=-1)
        elif kind == "m":
            outs = []
            for s in range(chunk // MEM_HEAD_DIM):
                ys = y[:, s * MEM_HEAD_DIM:(s + 1) * MEM_HEAD_DIM]
                outs.append(_rms_rows(ys, mw_ref[...]))
            y = jnp.concatenate(outs, axis=-1)
        o_ref[:, c0:c0 + chunk] = y.astype(BF16)


def _proj(x2, norm_w, w_in, gmat, qw, kw, mw, *, tm, segs, chunk):
    t, d = x2.shape
    n = w_in.shape[1]
    full = lambda shape: pl.BlockSpec(shape, lambda i: (0,) * len(shape))
    return pl.pallas_call(
        functools.partial(_proj_kernel, chunk=chunk, segs=segs),
        out_shape=jax.ShapeDtypeStruct((t, n), BF16),
        grid=(t // tm,),
        in_specs=[
            pl.BlockSpec((tm, d), lambda i: (i, 0)),
            full((1, d)),
            pl.BlockSpec((d, n), lambda i: (0, 0), pipeline_mode=pl.Buffered(1)),
            full(gmat.shape),
            full(qw.shape),
            full(kw.shape),
            full(mw.shape),
        ],
        out_specs=pl.BlockSpec((tm, n), lambda i: (i, 0)),
        scratch_shapes=[pltpu.VMEM((tm, d), BF16)],
        compiler_params=_cparams("parallel"),
        name="proj",
    )(x2, norm_w, w_in, gmat, qw, kw, mw)


def _pool_kernel(u_ref, pw_ref, ps_ref, o_ref):
    s_len = u_ref.shape[0]
    row = lax.broadcasted_iota(jnp.int32, (s_len, POOL_GROUP_DIM), 0)
    for g, w in enumerate(POOL_WINDOWS):
        cols = slice(g * POOL_GROUP_DIM, (g + 1) * POOL_GROUP_DIM)
        ug = u_ref[:, cols].astype(F32)
        acc = ug
        span = 1
        while span < w:
            shifted = pltpu.roll(acc, span, 0)
            acc = acc + jnp.where(row >= span, shifted, 0.0)
            span *= 2
        cnt = jnp.minimum(row + 1, w).astype(F32)
        pooled = acc / cnt - ug
        mixed = jnp.dot(pooled.astype(BF16), pw_ref[g], preferred_element_type=F32)
        o_ref[:, cols] = (mixed * ps_ref[:, cols]).astype(BF16)


def _pool(proj, pool_w, pool_scale, *, batch, seq):
    width = len(POOL_WINDOWS) * POOL_GROUP_DIM
    return pl.pallas_call(
        _pool_kernel,
        out_shape=jax.ShapeDtypeStruct((batch * seq, width), BF16),
        grid=(batch,),
        in_specs=[
            pl.BlockSpec((seq, width), lambda b: (b, 0)),
            pl.BlockSpec(pool_w.shape, lambda b: (0, 0, 0)),
            pl.BlockSpec((1, width), lambda b: (0, 0)),
        ],
        out_specs=pl.BlockSpec((seq, width), lambda b: (b, 0)),
        compiler_params=_cparams("parallel"),
        name="pool",
    )(proj, pool_w, pool_scale)


def _bucket_tables(blk):
    exact = NUM_BUCKETS // 2
    kj = np.arange(blk)[:, None]
    qi = np.arange(blk)[None, :]
    out = []
    for off in (0, 1):
        rel = qi - kj + off * blk
        n = np.maximum(rel, 0)
        nf = np.maximum(n, 1).astype(np.float64)
        large = exact + (np.log(nf / exact) / math.log(MAX_DISTANCE / exact) * (NUM_BUCKETS - exact)).astype(np.int64)
        large = np.minimum(large, NUM_BUCKETS - 1)
        bucket = np.where(n < exact, n, large)
        out.append(np.where(rel >= 0, bucket, -1))
    return np.stack(out).astype(np.int32)


def _bias_kernel(rb_ref, bucket_ref, o_ref):
    h = pl.program_id(0)
    far = rb_ref[(NUM_BUCKETS - 1) * DIFF_HEADS + h]
    for t in range(2):
        bk = bucket_ref[t]
        acc = jnp.full(bk.shape, -jnp.inf, F32)
        for b in range(NUM_BUCKETS):
            acc = jnp.where(bk == b, rb_ref[b * DIFF_HEADS + h] - far, acc)
        o_ref[0, t] = acc


def _bias_tables(rel_bias_flat, buckets):
    _, blk, _ = buckets.shape
    return pl.pallas_call(
        _bias_kernel,
        out_shape=jax.ShapeDtypeStruct((DIFF_HEADS, 2, blk, blk), F32),
        grid=(DIFF_HEADS,),
        in_specs=[
            pl.BlockSpec(memory_space=pltpu.SMEM),
            pl.BlockSpec(buckets.shape, lambda h: (0, 0, 0)),
        ],
        out_specs=pl.BlockSpec((1, 2, blk, blk), lambda h: (h, 0, 0, 0)),
        compiler_params=_cparams("arbitrary"),
        name="bias_tables",
    )(rel_bias_flat, buckets)


def _attn_kernel(q_ref, k_ref, v_ref, tbl_ref, lam_ref, nw_ref, o_ref, vt_ref, *, blk, hp):
    i = pl.program_id(2)
    nk = vt_ref.shape[1]
    heads = range(hp)
    hcols = lambda h: slice(h * LANES, (h + 1) * LANES)

    @pl.when(i == 0)
    def _():
        for h in heads:
            for c in range(nk):
                vt_ref[h, c] = v_ref[c * blk:(c + 1) * blk, hcols(h)].astype(F32).T.astype(BF16)

    lane = lax.broadcasted_iota(jnp.int32, (blk, LANES), 1)
    zero = jnp.zeros((blk, LANES), BF16)
    qq = []
    for h in heads:
        q = q_ref[:, hcols(h)]
        qq.append(jnp.concatenate([jnp.where(lane < DIFF_QK_DIM, q, zero),
                                   jnp.where(lane >= DIFF_QK_DIM, q, zero)], axis=0))

    def step(j, carry, biased):
        off = pl.multiple_of(j * blk, blk)
        scores = [lax.dot_general(k_ref[pl.ds(off, blk), hcols(h)], qq[h], NT_DIMS, preferred_element_type=F32)
                  for h in heads]
        stats, probs = [], []
        for h in heads:
            m, l, _ = carry[h]
            s = scores[h]
            if biased:
                bias = tbl_ref[h, i - j]
                s = s + jnp.concatenate([bias, bias], axis=1)
            m_new = jnp.maximum(m, jnp.max(s, axis=0, keepdims=True))
            p = jnp.exp(s - m_new)
            alpha = jnp.exp(m - m_new)
            stats.append((m_new, alpha * l + jnp.sum(p, axis=0, keepdims=True), alpha))
            probs.append(p.astype(BF16))
        pvs = [jnp.dot(vt_ref[h, j], probs[h], preferred_element_type=F32) for h in heads]
        return tuple((stats[h][0], stats[h][1], stats[h][2] * carry[h][2] + pvs[h]) for h in heads)

    init = tuple((jnp.full((1, 2 * blk), -jnp.inf, F32), jnp.zeros((1, 2 * blk), F32),
                  jnp.zeros((DIFF_V_DIM, 2 * blk), F32)) for _ in heads)
    carry = lax.fori_loop(0, i - 1, lambda j, c: step(j, c, False), init)
    carry = lax.fori_loop(jnp.maximum(i - 1, 0), i + 1, lambda j, c: step(j, c, True), carry)

    lv = lam_ref[...]
    lam = (jnp.exp(jnp.sum(lv[0:1] * lv[1:2], axis=-1, keepdims=True))
           - jnp.exp(jnp.sum(lv[2:3] * lv[3:4], axis=-1, keepdims=True)) + LAM_INIT)
    for h in heads:
        _, l, acc = carry[h]
        on = acc / l
        o = (on[:, :blk] - lam * on[:, blk:]).T
        o_ref[:, hcols(h)] = (_rms_rows(o, nw_ref[...]) * (1.0 - LAM_INIT)).astype(BF16)


def _diff_attention(proj, tables, lam_vecs, diff_norm_w, *, batch, seq, q_col, k_col, v_col, hp):
    blk = ATT_BLOCK
    nq = seq // blk
    width = hp * LANES
    qcb, kcb, vcb = q_col // width, k_col // width, v_col // width
    return pl.pallas_call(
        functools.partial(_attn_kernel, blk=blk, hp=hp),
        out_shape=jax.ShapeDtypeStruct((batch * seq, DIFF_HEADS * DIFF_V_DIM), BF16),
        grid=(batch, DIFF_HEADS // hp, nq),
        in_specs=[
            pl.BlockSpec((blk, width), lambda b, g, i: (b * nq + i, qcb + g)),
            pl.BlockSpec((seq, width), lambda b, g, i: (b, kcb + g)),
            pl.BlockSpec((seq, width), lambda b, g, i: (b, vcb + g)),
            pl.BlockSpec((hp, 2, blk, blk), lambda b, g, i: (g, 0, 0, 0)),
            pl.BlockSpec(lam_vecs.shape, lambda b, g, i: (0, 0)),
            pl.BlockSpec((1, DIFF_V_DIM), lambda b, g, i: (0, 0)),
        ],
        out_specs=pl.BlockSpec((blk, width), lambda b, g, i: (b * nq + i, g)),
        scratch_shapes=[pltpu.VMEM((hp, nq, DIFF_V_DIM, blk), BF16)],
        compiler_params=_cparams("arbitrary", "arbitrary", "arbitrary"),
        name="diff_attn",
    )(proj, proj, proj, tables, lam_vecs, diff_norm_w)


def _memkv_kernel(m_ref, nw_ref, w_ref, kw_ref, k_ref, v_ref):
    hm = _rms_rows(m_ref[...], nw_ref[...]).astype(BF16)
    kv = jnp.dot(hm, w_ref[...], preferred_element_type=F32)
    width = MEM_HEADS * MEM_HEAD_DIM
    for g in range(MEM_HEADS):
        cols = slice(g * MEM_HEAD_DIM, (g + 1) * MEM_HEAD_DIM)
        k_ref[:, cols] = _rms_rows(kv[:, cols], kw_ref[...]).astype(BF16)
    v_ref[...] = kv[:, width:].astype(BF16)


def _mem_kv(mem2, norm_w, w_kv, k_norm_w, *, batch, mlen):
    d = mem2.shape[1]
    width = MEM_HEADS * MEM_HEAD_DIM
    out = jax.ShapeDtypeStruct((batch * mlen, width), BF16)
    return pl.pallas_call(
        _memkv_kernel,
        out_shape=(out, out),
        grid=(batch,),
        in_specs=[
            pl.BlockSpec((mlen, d), lambda b: (b, 0)),
            pl.BlockSpec((1, d), lambda b: (0, 0)),
            pl.BlockSpec(w_kv.shape, lambda b: (0, 0)),
            pl.BlockSpec((1, MEM_HEAD_DIM), lambda b: (0, 0)),
        ],
        out_specs=(pl.BlockSpec((mlen, width), lambda b: (b, 0)),
                   pl.BlockSpec((mlen, width), lambda b: (b, 0))),
        compiler_params=_cparams("parallel"),
        name="mem_kv",
    )(mem2, norm_w, w_kv, k_norm_w)


def _memattn_kernel(q_ref, k_ref, v_ref, o_ref):
    for g in range(MEM_HEADS):
        cols = slice(g * MEM_HEAD_DIM, (g + 1) * MEM_HEAD_DIM)
        s = lax.dot_general(q_ref[:, cols], k_ref[:, cols], NT_DIMS, preferred_element_type=F32)
        s = s - jnp.max(s, axis=-1, keepdims=True)
        p = jnp.exp(s)
        p = p / jnp.sum(p, axis=-1, keepdims=True)
        o_ref[:, cols] = jnp.dot(p.astype(BF16), v_ref[:, cols], preferred_element_type=F32).astype(BF16)


def _mem_attention(proj, mk, mv, *, batch, seq, mlen, q_col, tm):
    width = MEM_HEADS * MEM_HEAD_DIM
    nt = seq // tm
    qcb = q_col // width
    return pl.pallas_call(
        _memattn_kernel,
        out_shape=jax.ShapeDtypeStruct((batch * seq, width), BF16),
        grid=(batch, nt),
        in_specs=[
            pl.BlockSpec((tm, width), lambda b, i: (b * nt + i, qcb)),
            pl.BlockSpec((mlen, width), lambda b, i: (b, 0)),
            pl.BlockSpec((mlen, width), lambda b, i: (b, 0)),
        ],
        out_specs=pl.BlockSpec((tm, width), lambda b, i: (b * nt + i, 0)),
        compiler_params=_cparams("parallel", "parallel"),
        name="mem_attn",
    )(proj, mk, mv)


def _merge_kernel(x_ref, nw_ref, pool_ref, diff_ref, mem_ref,
                  wg0_ref, wg1_ref, wg2_ref, bg0_ref, bg1_ref, bg2_ref,
                  wup_ref, wud_ref, wum_ref, wo_ref, o_ref, h_ref, acc_ref):
    n = pl.program_id(1)

    @pl.when(n == 0)
    def _():
        x = x_ref[...]
        h_ref[...] = _rms_rows(x, nw_ref[...]).astype(BF16)
        acc_ref[...] = x

    h = h_ref[...]

    def branch(a_ref, wg_ref, bg_ref, wu_ref):
        gate = jax.nn.sigmoid(jnp.dot(h, wg_ref[...], preferred_element_type=F32) + bg_ref[...])
        return gate * jnp.dot(a_ref[...], wu_ref[...], preferred_element_type=F32)

    y = (branch(pool_ref, wg0_ref, bg0_ref, wup_ref)
         + branch(diff_ref, wg1_ref, bg1_ref, wud_ref)
         + branch(mem_ref, wg2_ref, bg2_ref, wum_ref))
    acc_ref[...] += jnp.dot(y.astype(BF16), wo_ref[...], preferred_element_type=F32)

    @pl.when(n == pl.num_programs(1) - 1)
    def _():
        o_ref[...] = acc_ref[...]


def _merge(x2, norm_w, pool_out, diff_out, mem_out, w_gate, b_gate, w_up_pool, w_up_diff, w_up_mem, w_out,
           *, tm, tn):
    t, d = x2.shape
    nn = d // tn
    rows = lambda width: pl.BlockSpec((tm, width), lambda i, n: (i, 0))
    gate_w = lambda br: pl.BlockSpec((d, tn), lambda i, n: (0, br * nn + n))
    gate_b = lambda br: pl.BlockSpec((1, tn), lambda i, n: (0, br * nn + n))
    up_w = lambda width: pl.BlockSpec((width, tn), lambda i, n: (0, n))
    return pl.pallas_call(
        _merge_kernel,
        out_shape=jax.ShapeDtypeStruct((t, d), F32),
        grid=(t // tm, nn),
        in_specs=[
            rows(d),
            pl.BlockSpec((1, d), lambda i, n: (0, 0)),
            rows(pool_out.shape[1]), rows(diff_out.shape[1]), rows(mem_out.shape[1]),
            gate_w(0), gate_w(1), gate_w(2),
            gate_b(0), gate_b(1), gate_b(2),
            up_w(pool_out.shape[1]), up_w(diff_out.shape[1]), up_w(mem_out.shape[1]),
            pl.BlockSpec((tn, d), lambda i, n: (n, 0)),
        ],
        out_specs=pl.BlockSpec((tm, d), lambda i, n: (i, 0)),
        scratch_shapes=[pltpu.VMEM((tm, d), BF16), pltpu.VMEM((tm, d), F32)],
        compiler_params=_cparams("parallel", "arbitrary"),
        name="merge",
    )(x2, norm_w, pool_out, diff_out, mem_out, w_gate, w_gate, w_gate, b_gate, b_gate, b_gate,
      w_up_pool, w_up_diff, w_up_mem, w_out)


def _split_bf16(a):
    hi = a.astype(BF16)
    lo = (a - hi.astype(F32)).astype(BF16)
    return hi, lo


def _router_kernel(x_ref, nw_ref, whi_ref, wlo_ref, b_ref, idx_ref, gate_ref):
    h = _rms_rows(x_ref[...], nw_ref[...])
    hi, lo = _split_bf16(h)
    logits = (jnp.dot(hi, whi_ref[...], preferred_element_type=F32)
              + jnp.dot(hi, wlo_ref[...], preferred_element_type=F32)
              + jnp.dot(lo, whi_ref[...], preferred_element_type=F32)) + b_ref[...]
    tm, ne = logits.shape
    eid = lax.broadcasted_iota(jnp.int32, (tm, ne), 1)
    vals, ids = [], []
    cur = logits
    for _ in range(TOP_K):
        mx = jnp.max(cur, axis=-1, keepdims=True)
        sel = jnp.min(jnp.where(cur == mx, eid, ne), axis=-1, keepdims=True)
        vals.append(mx)
        ids.append(sel)
        cur = jnp.where(eid == sel, -jnp.inf, cur)
    exps = [jnp.exp(v - vals[0]) for v in vals]
    denom = exps[0] + exps[1] + exps[2] + exps[3]
    lane = lax.broadcasted_iota(jnp.int32, (tm, LANES), 1)
    idx_out = jnp.zeros((tm, LANES), jnp.int32)
    gate_out = jnp.zeros((tm, LANES), F32)
    for k in range(TOP_K):
        idx_out = jnp.where(lane == k, ids[k], idx_out)
        gate_out = jnp.where(lane == k, exps[k] / denom, gate_out)
    idx_ref[...] = idx_out
    gate_ref[...] = gate_out


def _router(x1, norm_w, w_hi, w_lo, bias, *, tm):
    t, d = x1.shape
    ne = w_hi.shape[1]
    return pl.pallas_call(
        _router_kernel,
        out_shape=(jax.ShapeDtypeStruct((t, LANES), jnp.int32), jax.ShapeDtypeStruct((t, LANES), F32)),
        grid=(t // tm,),
        in_specs=[
            pl.BlockSpec((tm, d), lambda i: (i, 0)),
            pl.BlockSpec((1, d), lambda i: (0, 0)),
            pl.BlockSpec((d, ne), lambda i: (0, 0)),
            pl.BlockSpec((d, ne), lambda i: (0, 0)),
            pl.BlockSpec((1, ne), lambda i: (0, 0)),
        ],
        out_specs=(pl.BlockSpec((tm, LANES), lambda i: (i, 0)),
                   pl.BlockSpec((tm, LANES), lambda i: (i, 0))),
        compiler_params=_cparams("parallel"),
        name="router",
    )(x1, norm_w, w_hi, w_lo, bias)


def _moe_kernel(be_ref, na_ref, tok_ref, tokn_ref, dstp_ref, x_hbm, nw_ref,
                wg_ref, wu_ref, bg_ref, bu_ref, wd_ref, bd_ref, y_hbm,
                xbuf, xn_ref, acc_ref, ybuf, gsem, ssem, *, tm, nf):
    b = pl.program_id(0)
    f = pl.program_id(1)
    nb = pl.num_programs(0)
    nact = na_ref[0]
    slot = lax.rem(b, 2)
    other = 1 - slot
    rows = tm // nf

    def gather_row(idx_ref, s, r):
        t = idx_ref[0, 0, r]
        pltpu.make_async_copy(x_hbm.at[pl.ds(t, 1)], xbuf.at[s, pl.ds(r, 1)], gsem.at[s]).start()

    def scatter_row(s, r):
        t = dstp_ref[0, 0, r]
        pltpu.make_async_copy(ybuf.at[s, pl.ds(r, 1)], y_hbm.at[pl.ds(t, 1)], ssem.at[s]).start()

    def gather_wait(s):
        pltpu.make_async_copy(x_hbm.at[pl.ds(0, tm)], xbuf.at[s], gsem.at[s]).wait()

    def scatter_wait(s):
        pltpu.make_async_copy(ybuf.at[s], y_hbm.at[pl.ds(0, tm)], ssem.at[s]).wait()

    def all_rows(fn):
        def one(r, c):
            fn(r)
            return c
        lax.fori_loop(0, tm, one, 0, unroll=8)

    active = b < nact

    @pl.when((b == 0) & (f == 0))
    def _():
        all_rows(lambda r: gather_row(tok_ref, 0, r))

    @pl.when((b == nact) & (f == 0))
    def _():
        all_rows(lambda r: scatter_row(other, r))

    @pl.when(active & (f == 0))
    def _():
        gather_wait(slot)
        xn_ref[...] = _rms_rows(xbuf[slot], nw_ref[...]).astype(BF16)

    @pl.when(active)
    def _():
        fetch_next = b + 1 < nact
        flush_prev = b >= 1
        for rr in range(rows):
            r = f * rows + rr

            @pl.when(fetch_next)
            def _():
                gather_row(tokn_ref, other, r)

            @pl.when(flush_prev)
            def _():
                scatter_row(other, r)

        xn = xn_ref[...]
        gate = jnp.dot(xn, wg_ref[0], preferred_element_type=F32) + bg_ref[0]
        up = jnp.dot(xn, wu_ref[0], preferred_element_type=F32) + bu_ref[0]
        gate = jnp.minimum(gate, SWIGLU_LIMIT)
        up = jnp.clip(up, -SWIGLU_LIMIT, SWIGLU_LIMIT)
        act = gate * jax.nn.sigmoid(gate * SWIGLU_ALPHA) * (up + 1.0)
        part = jnp.dot(act.astype(BF16), wd_ref[0], preferred_element_type=F32)
        acc_ref[...] = jnp.where(f == 0, part, acc_ref[...] + part)

    @pl.when(active & (f == nf - 1))
    def _():
        @pl.when(b >= 2)
        def _():
            scatter_wait(slot)

        ybuf[slot] = acc_ref[...] + bd_ref[0]

    @pl.when((b == nb - 1) & (f == nf - 1))
    def _():
        scatter_wait(lax.rem(nact - 1, 2))

        @pl.when(nact >= 2)
        def _():
            scatter_wait(lax.rem(nact, 2))


def _moe(blk_expert, nact, src_tok, dst_row, x1, norm_w, w_gu, b_gu, w_down, b_down, *, tm, tf, n_out_rows):
    t, d = x1.shape
    ne, _, two_f = w_gu.shape
    nf = two_f // 2 // tf
    nblk = src_tok.shape[0]

    def eidx(b, be, na):
        return be[jnp.minimum(b, na[0] - 1)]

    def fidx(b, f, na):
        return jnp.where(b < na[0], f, nf - 1)

    idx_blk = lambda shift: pl.BlockSpec(
        (1, 1, tm), lambda b, f, be, na: (jnp.clip(b + shift, 0, nblk - 1), 0, 0), memory_space=pltpu.SMEM)
    grid_spec = pltpu.PrefetchScalarGridSpec(
        num_scalar_prefetch=2,
        grid=(nblk, nf),
        in_specs=[
            idx_blk(0), idx_blk(1), idx_blk(-1),
            pl.BlockSpec(memory_space=pl.ANY),
            pl.BlockSpec((1, d), lambda b, f, be, na: (0, 0)),
            pl.BlockSpec((1, d, tf), lambda b, f, be, na: (eidx(b, be, na), 0, fidx(b, f, na))),
            pl.BlockSpec((1, d, tf), lambda b, f, be, na: (eidx(b, be, na), 0, nf + fidx(b, f, na))),
            pl.BlockSpec((1, 1, tf), lambda b, f, be, na: (eidx(b, be, na), 0, fidx(b, f, na))),
            pl.BlockSpec((1, 1, tf), lambda b, f, be, na: (eidx(b, be, na), 0, nf + fidx(b, f, na))),
            pl.BlockSpec((1, tf, d), lambda b, f, be, na: (eidx(b, be, na), fidx(b, f, na), 0)),
            pl.BlockSpec((1, 1, d), lambda b, f, be, na: (eidx(b, be, na), 0, 0)),
        ],
        out_specs=pl.BlockSpec(memory_space=pl.ANY),
        scratch_shapes=[
            pltpu.VMEM((2, tm, d), F32),
            pltpu.VMEM((tm, d), BF16),
            pltpu.VMEM((tm, d), F32),
            pltpu.VMEM((2, tm, d), F32),
            pltpu.SemaphoreType.DMA((2,)),
            pltpu.SemaphoreType.DMA((2,)),
        ],
    )
    return pl.pallas_call(
        functools.partial(_moe_kernel, tm=tm, nf=nf),
        out_shape=jax.ShapeDtypeStruct((n_out_rows, d), F32),
        grid_spec=grid_spec,
        compiler_params=_cparams("arbitrary", "arbitrary"),
        name="moe",
    )(blk_expert, nact, src_tok, src_tok, dst_row, x1, norm_w, w_gu, w_gu, b_gu, b_gu, w_down, b_down)


def _combine_kernel(x_ref, g_ref, y0_ref, y1_ref, y2_ref, y3_ref, o_ref):
    g = g_ref[...]
    out = x_ref[...]
    for k, y_ref in enumerate((y0_ref, y1_ref, y2_ref, y3_ref)):
        out = out + g[:, k:k + 1] * y_ref[...]
    o_ref[...] = out


def _combine(x1, gates, yk, *, tc):
    t, d = x1.shape
    nt = t // tc
    ysp = lambda k: pl.BlockSpec((tc, d), lambda i: (k * nt + i, 0))
    return pl.pallas_call(
        _combine_kernel,
        out_shape=jax.ShapeDtypeStruct((t, d), F32),
        grid=(nt,),
        in_specs=[
            pl.BlockSpec((tc, d), lambda i: (i, 0)),
            pl.BlockSpec((tc, LANES), lambda i: (i, 0)),
            ysp(0), ysp(1), ysp(2), ysp(3),
        ],
        out_specs=pl.BlockSpec((tc, d), lambda i: (i, 0)),
        compiler_params=_cparams("parallel"),
        name="combine",
    )(x1, gates, yk, yk, yk, yk)


def _routing_plan(top_idx, n_experts, tm):
    t = top_idx.shape[0]
    tk = t * TOP_K
    nblk = tk // tm + n_experts
    flat_e = top_idx.reshape(tk)
    order = jnp.argsort(flat_e).astype(jnp.int32)
    experts = jnp.arange(n_experts, dtype=jnp.int32)
    counts = jnp.sum((flat_e[:, None] == experts[None, :]).astype(jnp.int32), axis=0)
    padded = ((counts + tm - 1) // tm) * tm
    start = jnp.cumsum(counts) - counts
    pcum = jnp.cumsum(padded)
    pstart = pcum - padded
    blk_start = jnp.arange(nblk, dtype=jnp.int32) * tm
    blk_expert = jnp.minimum(jnp.sum((pcum[None, :] <= blk_start[:, None]).astype(jnp.int32), axis=1),
                             n_experts - 1)
    rank = (blk_start - pstart[blk_expert])[:, None] + jnp.arange(tm, dtype=jnp.int32)[None, :]
    cnt = counts[blk_expert][:, None]
    valid = rank < cnt
    row_flat = order[jnp.minimum(start[blk_expert][:, None] + rank, tk - 1)]
    tok = row_flat // TOP_K
    kk = row_flat - tok * TOP_K
    src_tok = jnp.where(valid, tok, 0)
    dst_row = jnp.where(valid, kk * t + tok, tk + blk_expert[:, None] * tm + (rank - cnt))
    nact = (pcum[-1] // tm).astype(jnp.int32).reshape(1)
    return (blk_expert, nact, src_tok.reshape(nblk, 1, tm), dst_row.reshape(nblk, 1, tm), tk + n_experts * tm)


def kernel(x, mem, norm_mix_w, norm_mem_w, w_in, w_gate, b_gate, pool_w, pool_scale, q_norm_w, k_norm_w,
           lambda_q1, lambda_k1, lambda_q2, lambda_k2, diff_norm_w, rel_bias, w_mem_kv, mem_q_norm_w,
           mem_k_norm_w, w_up_pool, w_up_diff, w_up_mem, w_out, norm_ffn_w, router_w, router_b, w_gu, b_gu,
           w_down, b_down):
    batch, seq, d = x.shape
    mlen = mem.shape[1]
    t = batch * seq
    assert norm_mix_w.shape[0] == 1, "single-layer block"
    n_experts = router_w.shape[2]

    pool_width = len(POOL_WINDOWS) * POOL_GROUP_DIM
    qk_width = DIFF_HEADS * 2 * DIFF_QK_DIM
    v_width = DIFF_HEADS * DIFF_V_DIM
    mem_width = MEM_HEADS * MEM_HEAD_DIM
    q_col = pool_width
    k_col = q_col + qk_width
    v_col = k_col + qk_width
    mq_col = v_col + v_width
    chunk = 512
    assert w_in.shape[2] == mq_col + mem_width
    segs = (("p",) * (pool_width // chunk) + ("q",) * (qk_width // chunk) + ("k",) * (qk_width // chunk)
            + ("v",) * (v_width // chunk) + ("m",) * (mem_width // chunk))

    x2 = x.reshape(t, d)
    row = lambda a: a.reshape(1, -1)

    half = chunk // 2
    gmat = jnp.asarray(np.kron(np.eye(half // DIFF_QK_DIM), np.ones((DIFF_QK_DIM, DIFF_QK_DIM))), BF16)
    reps = chunk // DIFF_QK_DIM
    qw = row(jnp.tile(q_norm_w[0] * (1.0 / math.sqrt(DIFF_QK_DIM)), reps))
    kw = row(jnp.tile(k_norm_w[0], reps))
    mw = row(mem_q_norm_w[0] * (1.0 / math.sqrt(MEM_HEAD_DIM)))
    proj = _proj(x2, row(norm_mix_w[0]), w_in[0].astype(BF16), gmat, qw, kw, mw, tm=512, segs=segs, chunk=chunk)

    pool_out = _pool(proj, pool_w[0].astype(BF16), row(pool_scale[0]), batch=batch, seq=seq)

    rel_flat = rel_bias.reshape(-1)
    tables = _bias_tables(rel_flat, jnp.asarray(_bucket_tables(ATT_BLOCK)))
    lam_vecs = jnp.concatenate([lambda_q1, lambda_k1, lambda_q2, lambda_k2], axis=0)
    diff_out = _diff_attention(proj, tables, lam_vecs, row(diff_norm_w[0]),
                               batch=batch, seq=seq, q_col=q_col, k_col=k_col, v_col=v_col, hp=4)

    mk, mv = _mem_kv(mem.reshape(batch * mlen, d), row(norm_mem_w[0]), w_mem_kv[0].astype(BF16),
                     row(mem_k_norm_w[0]), batch=batch, mlen=mlen)
    mem_out = _mem_attention(proj, mk, mv, batch=batch, seq=seq, mlen=mlen, q_col=mq_col, tm=512)

    x1 = _merge(x2, row(norm_mix_w[0]), pool_out, diff_out, mem_out, w_gate[0].astype(BF16), row(b_gate[0]),
                w_up_pool[0].astype(BF16), w_up_diff[0].astype(BF16), w_up_mem[0].astype(BF16),
                w_out[0].astype(BF16), tm=512, tn=512)

    rw = router_w[0]
    rw_hi = rw.astype(BF16)
    rw_lo = (rw - rw_hi.astype(F32)).astype(BF16)
    idx_pad, gate_pad = _router(x1, row(norm_ffn_w[0]), rw_hi, rw_lo, row(router_b[0]), tm=512)
    tm_moe = 512
    blk_expert, nact, src_tok, dst_row, n_out_rows = _routing_plan(idx_pad[:, :TOP_K], n_experts, tm_moe)
    yk = _moe(blk_expert, nact, src_tok, dst_row, x1, row(norm_ffn_w[0]),
              w_gu[0].astype(BF16), b_gu[0].reshape(n_experts, 1, -1),
              w_down[0].astype(BF16), b_down[0].reshape(n_experts, 1, -1),
              tm=tm_moe, tf=512, n_out_rows=n_out_rows)
    out = _combine(x1, gate_pad, yk, tc=256)
    return out.reshape(batch, seq, d)
```

```python
import functools
import math

import numpy as np
import jax
import jax.numpy as jnp
from jax import lax
from jax.experimental import pallas as pl
from jax.experimental.pallas import tpu as pltpu

F32 = jnp.float32
BF16 = jnp.bfloat16

EPS = 1e-6
DIFF_HEADS = 8
DIFF_QK_DIM = 64
DIFF_V_DIM = 128
MEM_HEADS = 4
MEM_HEAD_DIM = 128
POOL_WINDOWS = (2, 4, 8, 16)
POOL_GROUP_DIM = 128
NUM_BUCKETS = 32
MAX_DISTANCE = 128
TOP_K = 4
SWIGLU_LIMIT = 7.0
SWIGLU_ALPHA = 1.702
LAM_INIT = 0.8 - 0.6 * math.exp(-0.3 * 0)

LANES = 128
VMEM_LIMIT = 56 * 1024 * 1024

ATT_BLOCK = 256
NT_DIMS = (((1,), (1,)), ((), ()))


def _cparams(*sem):
    return pltpu.CompilerParams(dimension_semantics=sem, vmem_limit_bytes=VMEM_LIMIT)


def _rms_rows(x, w):
    ms = jnp.mean(x * x, axis=-1, keepdims=True)
    return x * lax.rsqrt(ms + EPS) * w


def _proj_kernel(x_ref, nw_ref, w_ref, g_ref, qw_ref, kw_ref, mw_ref, o_ref, h_ref, *, chunk, segs):
    h_ref[...] = _rms_rows(x_ref[...], nw_ref[...]).astype(BF16)
    n = w_ref.shape[1]
    half = chunk // 2
    for c in range(n // chunk):
        c0 = c * chunk
        y = jnp.dot(h_ref[...], w_ref[:, c0:c0 + chunk], preferred_element_type=F32)
        kind = segs[c]
        if kind in ("q", "k"):
            wn = (qw_ref if kind == "q" else kw_ref)[...]
            outs = []
            for s in range(2):
                ys = y[:, s * half:(s + 1) * half]
                ss = jnp.dot((ys * ys).astype(BF16), g_ref[...], preferred_element_type=F32)
                outs.append(ys * lax.rsqrt(ss * (1.0 / DIFF_QK_DIM) + EPS) * wn[:, s * half:(s + 1) * half])
            y = jnp.concatenate(outs, axis=-1)
        elif kind == "m":
            outs = []
            for s in range(chunk // MEM_HEAD_DIM):
                ys = y[:, s * MEM_HEAD_DIM:(s + 1) * MEM_HEAD_DIM]
                outs.append(_rms_rows(ys, mw_ref[...]))
            y = jnp.concatenate(outs, axis=-1)
        o_ref[:, c0:c0 + chunk] = y.astype(BF16)


def _proj(x2, norm_w, w_in, gmat, qw, kw, mw, *, tm, segs, chunk):
    t, d = x2.shape
    n = w_in.shape[1]
    full = lambda shape: pl.BlockSpec(shape, lambda i: (0,) * len(shape))
    return pl.pallas_call(
        functools.partial(_proj_kernel, chunk=chunk, segs=segs),
        out_shape=jax.ShapeDtypeStruct((t, n), BF16),
        grid=(t // tm,),
        in_specs=[
            pl.BlockSpec((tm, d), lambda i: (i, 0)),
            full((1, d)),
            pl.BlockSpec((d, n), lambda i: (0, 0), pipeline_mode=pl.Buffered(1)),
            full(gmat.shape),
            full(qw.shape),
            full(kw.shape),
            full(mw.shape),
        ],
        out_specs=pl.BlockSpec((tm, n), lambda i: (i, 0)),
        scratch_shapes=[pltpu.VMEM((tm, d), BF16)],
        compiler_params=_cparams("parallel"),
        name="proj",
    )(x2, norm_w, w_in, gmat, qw, kw, mw)


def _pool_kernel(u_ref, pw_ref, ps_ref, o_ref):
    s_len = u_ref.shape[0]
    row = lax.broadcasted_iota(jnp.int32, (s_len, POOL_GROUP_DIM), 0)
    for g, w in enumerate(POOL_WINDOWS):
        cols = slice(g * POOL_GROUP_DIM, (g + 1) * POOL_GROUP_DIM)
        ug = u_ref[:, cols].astype(F32)
        acc = ug
        span = 1
        while span < w:
            shifted = pltpu.roll(acc, span, 0)
            acc = acc + jnp.where(row >= span, shifted, 0.0)
            span *= 2
        cnt = jnp.minimum(row + 1, w).astype(F32)
        pooled = acc / cnt - ug
        mixed = jnp.dot(pooled.astype(BF16), pw_ref[g], preferred_element_type=F32)
        o_ref[:, cols] = (mixed * ps_ref[:, cols]).astype(BF16)


def _pool(proj, pool_w, pool_scale, *, batch, seq):
    width = len(POOL_WINDOWS) * POOL_GROUP_DIM
    return pl.pallas_call(
        _pool_kernel,
        out_shape=jax.ShapeDtypeStruct((batch * seq, width), BF16),
        grid=(batch,),
        in_specs=[
            pl.BlockSpec((seq, width), lambda b: (b, 0)),
            pl.BlockSpec(pool_w.shape, lambda b: (0, 0, 0)),
            pl.BlockSpec((1, width), lambda b: (0, 0)),
        ],
        out_specs=pl.BlockSpec((seq, width), lambda b: (b, 0)),
        compiler_params=_cparams("parallel"),
        name="pool",
    )(proj, pool_w, pool_scale)


def _bucket_tables(blk):
    exact = NUM_BUCKETS // 2
    kj = np.arange(blk)[:, None]
    qi = np.arange(blk)[None, :]
    out = []
    for off in (0, 1):
        rel = qi - kj + off * blk
        n = np.maximum(rel, 0)
        nf = np.maximum(n, 1).astype(np.float64)
        large = exact + (np.log(nf / exact) / math.log(MAX_DISTANCE / exact) * (NUM_BUCKETS - exact)).astype(np.int64)
        large = np.minimum(large, NUM_BUCKETS - 1)
        bucket = np.where(n < exact, n, large)
        out.append(np.where(rel >= 0, bucket, -1))
    return np.stack(out).astype(np.int32)


def _bias_kernel(rb_ref, bucket_ref, o_ref):
    h = pl.program_id(0)
    far = rb_ref[(NUM_BUCKETS - 1) * DIFF_HEADS + h]
    for t in range(2):
        bk = bucket_ref[t]
        acc = jnp.full(bk.shape, -jnp.inf, F32)
        for b in range(NUM_BUCKETS):
            acc = jnp.where(bk == b, rb_ref[b * DIFF_HEADS + h] - far, acc)
        o_ref[0, t] = acc


def _bias_tables(rel_bias_flat, buckets):
    _, blk, _ = buckets.shape
    return pl.pallas_call(
        _bias_kernel,
        out_shape=jax.ShapeDtypeStruct((DIFF_HEADS, 2, blk, blk), F32),
        grid=(DIFF_HEADS,),
        in_specs=[
            pl.BlockSpec(memory_space=pltpu.SMEM),
            pl.BlockSpec(buckets.shape, lambda h: (0, 0, 0)),
        ],
        out_specs=pl.BlockSpec((1, 2, blk, blk), lambda h: (h, 0, 0, 0)),
        compiler_params=_cparams("arbitrary"),
        name="bias_tables",
    )(rel_bias_flat, buckets)


def _attn_kernel(q_ref, k_ref, v_ref, tbl_ref, lam_ref, nw_ref, o_ref, vt_ref, *, blk, hp):
    i = pl.program_id(2)
    nk = vt_ref.shape[1]
    heads = range(hp)
    hcols = lambda h: slice(h * LANES, (h + 1) * LANES)

    @pl.when(i == 0)
    def _():
        for h in heads:
            for c in range(nk):
                vt_ref[h, c] = v_ref[c * blk:(c + 1) * blk, hcols(h)].astype(F32).T.astype(BF16)

    lane = lax.broadcasted_iota(jnp.int32, (blk, LANES), 1)
    zero = jnp.zeros((blk, LANES), BF16)
    qq = []
    for h in heads:
        q = q_ref[:, hcols(h)]
        qq.append(jnp.concatenate([jnp.where(lane < DIFF_QK_DIM, q, zero),
                                   jnp.where(lane >= DIFF_QK_DIM, q, zero)], axis=0))

    def step(j, carry, biased):
        off = pl.multiple_of(j * blk, blk)
        scores = [lax.dot_general(k_ref[pl.ds(off, blk), hcols(h)], qq[h], NT_DIMS, preferred_element_type=F32)
                  for h in heads]
        stats, probs = [], []
        for h in heads:
            m, l, _ = carry[h]
            s = scores[h]
            if biased:
                bias = tbl_ref[h, i - j]
                s = s + jnp.concatenate([bias, bias], axis=1)
            m_new = jnp.maximum(m, jnp.max(s, axis=0, keepdims=True))
            p = jnp.exp(s - m_new)
            alpha = jnp.exp(m - m_new)
            stats.append((m_new, alpha * l + jnp.sum(p, axis=0, keepdims=True), alpha))
            probs.append(p.astype(BF16))
        pvs = [jnp.dot(vt_ref[h, j], probs[h], preferred_element_type=F32) for h in heads]
        return tuple((stats[h][0], stats[h][1], stats[h][2] * carry[h][2] + pvs[h]) for h in heads)

    init = tuple((jnp.full((1, 2 * blk), -jnp.inf, F32), jnp.zeros((1, 2 * blk), F32),
                  jnp.zeros((DIFF_V_DIM, 2 * blk), F32)) for _ in heads)
    carry = lax.fori_loop(0, i - 1, lambda j, c: step(j, c, False), init)
    carry = lax.fori_loop(jnp.maximum(i - 1, 0), i + 1, lambda j, c: step(j, c, True), carry)

    lv = lam_ref[...]
    lam = (jnp.exp(jnp.sum(lv[0:1] * lv[1:2], axis=-1, keepdims=True))
           - jnp.exp(jnp.sum(lv[2:3] * lv[3:4], axis=-1, keepdims=True)) + LAM_INIT)
    for h in heads:
        _, l, acc = carry[h]
        on = acc / l
        o = (on[:, :blk] - lam * on[:, blk:]).T
        o_ref[:, hcols(h)] = (_rms_rows(o, nw_ref[...]) * (1.0 - LAM_INIT)).astype(BF16)


def _diff_attention(proj, tables, lam_vecs, diff_norm_w, *, batch, seq, q_col, k_col, v_col, hp):
    blk = ATT_BLOCK
    nq = seq // blk
    width = hp * LANES
    qcb, kcb, vcb = q_col // width, k_col // width, v_col // width
    return pl.pallas_call(
        functools.partial(_attn_kernel, blk=blk, hp=hp),
        out_shape=jax.ShapeDtypeStruct((batch * seq, DIFF_HEADS * DIFF_V_DIM), BF16),
        grid=(batch, DIFF_HEADS // hp, nq),
        in_specs=[
            pl.BlockSpec((blk, width), lambda b, g, i: (b * nq + i, qcb + g)),
            pl.BlockSpec((seq, width), lambda b, g, i: (b, kcb + g)),
            pl.BlockSpec((seq, width), lambda b, g, i: (b, vcb + g)),
            pl.BlockSpec((hp, 2, blk, blk), lambda b, g, i: (g, 0, 0, 0)),
            pl.BlockSpec(lam_vecs.shape, lambda b, g, i: (0, 0)),
            pl.BlockSpec((1, DIFF_V_DIM), lambda b, g, i: (0, 0)),
        ],
        out_specs=pl.BlockSpec((blk, width), lambda b, g, i: (b * nq + i, g)),
        scratch_shapes=[pltpu.VMEM((hp, nq, DIFF_V_DIM, blk), BF16)],
        compiler_params=_cparams("arbitrary", "arbitrary", "arbitrary"),
        name="diff_attn",
    )(proj, proj, proj, tables, lam_vecs, diff_norm_w)


def _memkv_kernel(m_ref, nw_ref, w_ref, kw_ref, k_ref, v_ref):
    hm = _rms_rows(m_ref[...], nw_ref[...]).astype(BF16)
    kv = jnp.dot(hm, w_ref[...], preferred_element_type=F32)
    width = MEM_HEADS * MEM_HEAD_DIM
    for g in range(MEM_HEADS):
        cols = slice(g * MEM_HEAD_DIM, (g + 1) * MEM_HEAD_DIM)
        k_ref[:, cols] = _rms_rows(kv[:, cols], kw_ref[...]).astype(BF16)
    v_ref[...] = kv[:, width:].astype(BF16)


def _mem_kv(mem2, norm_w, w_kv, k_norm_w, *, batch, mlen):
    d = mem2.shape[1]
    width = MEM_HEADS * MEM_HEAD_DIM
    out = jax.ShapeDtypeStruct((batch * mlen, width), BF16)
    return pl.pallas_call(
        _memkv_kernel,
        out_shape=(out, out),
        grid=(batch,),
        in_specs=[
            pl.BlockSpec((mlen, d), lambda b: (b, 0)),
            pl.BlockSpec((1, d), lambda b: (0, 0)),
            pl.BlockSpec(w_kv.shape, lambda b: (0, 0)),
            pl.BlockSpec((1, MEM_HEAD_DIM), lambda b: (0, 0)),
        ],
        out_specs=(pl.BlockSpec((mlen, width), lambda b: (b, 0)),
                   pl.BlockSpec((mlen, width), lambda b: (b, 0))),
        compiler_params=_cparams("parallel"),
        name="mem_kv",
    )(mem2, norm_w, w_kv, k_norm_w)


def _memattn_kernel(q_ref, k_ref, v_ref, o_ref):
    for g in range(MEM_HEADS):
        cols = slice(g * MEM_HEAD_DIM, (g + 1) * MEM_HEAD_DIM)
        s = lax.dot_general(q_ref[:, cols], k_ref[:, cols], NT_DIMS, preferred_element_type=F32)
        s = s - jnp.max(s, axis=-1, keepdims=True)
        p = jnp.exp(s)
        p = p / jnp.sum(p, axis=-1, keepdims=True)
        o_ref[:, cols] = jnp.dot(p.astype(BF16), v_ref[:, cols], preferred_element_type=F32).astype(BF16)


def _mem_attention(proj, mk, mv, *, batch, seq, mlen, q_col, tm):
    width = MEM_HEADS * MEM_HEAD_DIM
    nt = seq // tm
    qcb = q_col // width
    return pl.pallas_call(
        _memattn_kernel,
        out_shape=jax.ShapeDtypeStruct((batch * seq, width), BF16),
        grid=(batch, nt),
        in_specs=[
            pl.BlockSpec((tm, width), lambda b, i: (b * nt + i, qcb)),
            pl.BlockSpec((mlen, width), lambda b, i: (b, 0)),
            pl.BlockSpec((mlen, width), lambda b, i: (b, 0)),
        ],
        out_specs=pl.BlockSpec((tm, width), lambda b, i: (b * nt + i, 0)),
        compiler_params=_cparams("parallel", "parallel"),
        name="mem_attn",
    )(proj, mk, mv)


def _merge_kernel(x_ref, nw_ref, pool_ref, diff_ref, mem_ref,
                  wg0_ref, wg1_ref, wg2_ref, bg0_ref, bg1_ref, bg2_ref,
                  wup_ref, wud_ref, wum_ref, wo_ref, o_ref, h_ref, acc_ref):
    n = pl.program_id(1)

    @pl.when(n == 0)
    def _():
        x = x_ref[...]
        h_ref[...] = _rms_rows(x, nw_ref[...]).astype(BF16)
        acc_ref[...] = x

    h = h_ref[...]

    def branch(a_ref, wg_ref, bg_ref, wu_ref):
        gate = jax.nn.sigmoid(jnp.dot(h, wg_ref[...], preferred_element_type=F32) + bg_ref[...])
        return gate * jnp.dot(a_ref[...], wu_ref[...], preferred_element_type=F32)

    y = (branch(pool_ref, wg0_ref, bg0_ref, wup_ref)
         + branch(diff_ref, wg1_ref, bg1_ref, wud_ref)
         + branch(mem_ref, wg2_ref, bg2_ref, wum_ref))
    acc_ref[...] += jnp.dot(y.astype(BF16), wo_ref[...], preferred_element_type=F32)

    @pl.when(n == pl.num_programs(1) - 1)
    def _():
        o_ref[...] = acc_ref[...]


def _merge(x2, norm_w, pool_out, diff_out, mem_out, w_gate, b_gate, w_up_pool, w_up_diff, w_up_mem, w_out,
           *, tm, tn):
    t, d = x2.shape
    nn = d // tn
    rows = lambda width: pl.BlockSpec((tm, width), lambda i, n: (i, 0))
    gate_w = lambda br: pl.BlockSpec((d, tn), lambda i, n: (0, br * nn + n))
    gate_b = lambda br: pl.BlockSpec((1, tn), lambda i, n: (0, br * nn + n))
    up_w = lambda width: pl.BlockSpec((width, tn), lambda i, n: (0, n))
    return pl.pallas_call(
        _merge_kernel,
        out_shape=jax.ShapeDtypeStruct((t, d), F32),
        grid=(t // tm, nn),
        in_specs=[
            rows(d),
            pl.BlockSpec((1, d), lambda i, n: (0, 0)),
            rows(pool_out.shape[1]), rows(diff_out.shape[1]), rows(mem_out.shape[1]),
            gate_w(0), gate_w(1), gate_w(2),
            gate_b(0), gate_b(1), gate_b(2),
            up_w(pool_out.shape[1]), up_w(diff_out.shape[1]), up_w(mem_out.shape[1]),
            pl.BlockSpec((tn, d), lambda i, n: (n, 0)),
        ],
        out_specs=pl.BlockSpec((tm, d), lambda i, n: (i, 0)),
        scratch_shapes=[pltpu.VMEM((tm, d), BF16), pltpu.VMEM((tm, d), F32)],
        compiler_params=_cparams("parallel", "arbitrary"),
        name="merge",
    )(x2, norm_w, pool_out, diff_out, mem_out, w_gate, w_gate, w_gate, b_gate, b_gate, b_gate,
      w_up_pool, w_up_diff, w_up_mem, w_out)


def _split_bf16(a):
    hi = a.astype(BF16)
    lo = (a - hi.astype(F32)).astype(BF16)
    return hi, lo


def _router_kernel(x_ref, nw_ref, whi_ref, wlo_ref, b_ref, idx_ref, gate_ref):
    h = _rms_rows(x_ref[...], nw_ref[...])
    hi, lo = _split_bf16(h)
    logits = (jnp.dot(hi, whi_ref[...], preferred_element_type=F32)
              + jnp.dot(hi, wlo_ref[...], preferred_element_type=F32)
              + jnp.dot(lo, whi_ref[...], preferred_element_type=F32)) + b_ref[...]
    tm, ne = logits.shape
    eid = lax.broadcasted_iota(jnp.int32, (tm, ne), 1)
    vals, ids = [], []
    cur = logits
    for _ in range(TOP_K):
        mx = jnp.max(cur, axis=-1, keepdims=True)
        sel = jnp.min(jnp.where(cur == mx, eid, ne), axis=-1, keepdims=True)
        vals.append(mx)
        ids.append(sel)
        cur = jnp.where(eid == sel, -jnp.inf, cur)
    exps = [jnp.exp(v - vals[0]) for v in vals]
    denom = exps[0] + exps[1] + exps[2] + exps[3]
    lane = lax.broadcasted_iota(jnp.int32, (tm, LANES), 1)
    idx_out = jnp.zeros((tm, LANES), jnp.int32)
    gate_out = jnp.zeros((tm, LANES), F32)
    for k in range(TOP_K):
        idx_out = jnp.where(lane == k, ids[k], idx_out)
        gate_out = jnp.where(lane == k, exps[k] / denom, gate_out)
    idx_ref[...] = idx_out
    gate_ref[...] = gate_out


def _router(x1, norm_w, w_hi, w_lo, bias, *, tm):
    t, d = x1.shape
    ne = w_hi.shape[1]
    return pl.pallas_call(
        _router_kernel,
        out_shape=(jax.ShapeDtypeStruct((t, LANES), jnp.int32), jax.ShapeDtypeStruct((t, LANES), F32)),
        grid=(t // tm,),
        in_specs=[
            pl.BlockSpec((tm, d), lambda i: (i, 0)),
            pl.BlockSpec((1, d), lambda i: (0, 0)),
            pl.BlockSpec((d, ne), lambda i: (0, 0)),
            pl.BlockSpec((d, ne), lambda i: (0, 0)),
            pl.BlockSpec((1, ne), lambda i: (0, 0)),
        ],
        out_specs=(pl.BlockSpec((tm, LANES), lambda i: (i, 0)),
                   pl.BlockSpec((tm, LANES), lambda i: (i, 0))),
        compiler_params=_cparams("parallel"),
        name="router",
    )(x1, norm_w, w_hi, w_lo, bias)


def _moe_kernel(be_ref, na_ref, tok_ref, tokn_ref, dstp_ref, x_hbm, nw_ref,
                wg_ref, wu_ref, bg_ref, bu_ref, wd_ref, bd_ref, y_hbm,
                xbuf, xn_ref, acc_ref, ybuf, gsem, ssem, *, tm, nf):
    b = pl.program_id(0)
    f = pl.program_id(1)
    nb = pl.num_programs(0)
    nact = na_ref[0]
    slot = lax.rem(b, 2)
    other = 1 - slot

    def gather_row(idx_ref, s, r):
        t = idx_ref[0, 0, r]
        pltpu.make_async_copy(x_hbm.at[pl.ds(t, 1)], xbuf.at[s, pl.ds(r, 1)], gsem.at[s]).start()

    def scatter_row(s, r):
        t = dstp_ref[0, 0, r]
        pltpu.make_async_copy(ybuf.at[s, pl.ds(r, 1)], y_hbm.at[pl.ds(t, 1)], ssem.at[s]).start()

    def gather_wait(s):
        pltpu.make_async_copy(x_hbm.at[pl.ds(0, tm)], xbuf.at[s], gsem.at[s]).wait()

    def scatter_wait(s):
        pltpu.make_async_copy(ybuf.at[s], y_hbm.at[pl.ds(0, tm)], ssem.at[s]).wait()

    def all_rows(fn):
        def one(r, c):
            fn(r)
            return c
        lax.fori_loop(0, tm, one, 0, unroll=8)

    active = b < nact

    @pl.when((b == 0) & (f == 0))
    def _():
        all_rows(lambda r: gather_row(tok_ref, 0, r))

    @pl.when((b == nact) & (f == 0))
    def _():
        all_rows(lambda r: scatter_row(other, r))

    @pl.when(active & (f == 0))
    def _():
        gather_wait(slot)
        xn_ref[...] = _rms_rows(xbuf[slot], nw_ref[...]).astype(BF16)

    for s in (0, 1):
        @pl.when(active & (f == 1) & (b + 1 < nact) & (other == s))
        def _():
            for r in range(tm):
                gather_row(tokn_ref, s, r)

        @pl.when(active & (f == 2) & (b >= 1) & (other == s))
        def _():
            for r in range(tm):
                scatter_row(s, r)

    @pl.when(active)
    def _():
        xn = xn_ref[...]
        gate = jnp.dot(xn, wg_ref[0].astype(BF16), preferred_element_type=F32) + bg_ref[0]
        up = jnp.dot(xn, wu_ref[0].astype(BF16), preferred_element_type=F32) + bu_ref[0]
        gate = jnp.minimum(gate, SWIGLU_LIMIT)
        up = jnp.clip(up, -SWIGLU_LIMIT, SWIGLU_LIMIT)
        act = gate * jax.nn.sigmoid(gate * SWIGLU_ALPHA) * (up + 1.0)
        part = jnp.dot(act.astype(BF16), wd_ref[0].astype(BF16), preferred_element_type=F32)
        acc_ref[...] = jnp.where(f == 0, part, acc_ref[...] + part)

    @pl.when(active & (f == nf - 1))
    def _():
        @pl.when(b >= 2)
        def _():
            scatter_wait(slot)

        ybuf[slot] = acc_ref[...] + bd_ref[0]

    @pl.when((b == nb - 1) & (f == nf - 1))
    def _():
        scatter_wait(lax.rem(nact - 1, 2))

        @pl.when(nact >= 2)
        def _():
            scatter_wait(lax.rem(nact, 2))


def _moe(blk_expert, nact, src_tok, dst_row, x1, norm_w, w_gu, b_gu, w_down, b_down, *, tm, tf, n_out_rows):
    t, d = x1.shape
    ne, _, two_f = w_gu.shape
    nf = two_f // 2 // tf
    assert nf >= 3, "row DMAs are issued in d_ff steps 1 and 2"
    nblk = src_tok.shape[0]

    def eidx(b, be, na):
        return be[jnp.minimum(b, na[0] - 1)]

    def fidx(b, f, na):
        return jnp.where(b < na[0], f, nf - 1)

    idx_blk = lambda shift: pl.BlockSpec(
        (1, 1, tm), lambda b, f, be, na: (jnp.clip(b + shift, 0, nblk - 1), 0, 0), memory_space=pltpu.SMEM)
    grid_spec = pltpu.PrefetchScalarGridSpec(
        num_scalar_prefetch=2,
        grid=(nblk, nf),
        in_specs=[
            idx_blk(0), idx_blk(1), idx_blk(-1),
            pl.BlockSpec(memory_space=pl.ANY),
            pl.BlockSpec((1, d), lambda b, f, be, na: (0, 0)),
            pl.BlockSpec((1, d, tf), lambda b, f, be, na: (eidx(b, be, na), 0, fidx(b, f, na))),
            pl.BlockSpec((1, d, tf), lambda b, f, be, na: (eidx(b, be, na), 0, nf + fidx(b, f, na))),
            pl.BlockSpec((1, 1, tf), lambda b, f, be, na: (eidx(b, be, na), 0, fidx(b, f, na))),
            pl.BlockSpec((1, 1, tf), lambda b, f, be, na: (eidx(b, be, na), 0, nf + fidx(b, f, na))),
            pl.BlockSpec((1, tf, d), lambda b, f, be, na: (eidx(b, be, na), fidx(b, f, na), 0)),
            pl.BlockSpec((1, 1, d), lambda b, f, be, na: (eidx(b, be, na), 0, 0)),
        ],
        out_specs=pl.BlockSpec(memory_space=pl.ANY),
        scratch_shapes=[
            pltpu.VMEM((2, tm, d), F32),
            pltpu.VMEM((tm, d), BF16),
            pltpu.VMEM((tm, d), F32),
            pltpu.VMEM((2, tm, d), F32),
            pltpu.SemaphoreType.DMA((2,)),
            pltpu.SemaphoreType.DMA((2,)),
        ],
    )
    return pl.pallas_call(
        functools.partial(_moe_kernel, tm=tm, nf=nf),
        out_shape=jax.ShapeDtypeStruct((n_out_rows, d), F32),
        grid_spec=grid_spec,
        compiler_params=_cparams("arbitrary", "arbitrary"),
        name="moe",
    )(blk_expert, nact, src_tok, src_tok, dst_row, x1, norm_w, w_gu, w_gu, b_gu, b_gu, w_down, b_down)


def _combine_kernel(x_ref, g_ref, y0_ref, y1_ref, y2_ref, y3_ref, o_ref):
    g = g_ref[...]
    out = x_ref[...]
    for k, y_ref in enumerate((y0_ref, y1_ref, y2_ref, y3_ref)):
        out = out + g[:, k:k + 1] * y_ref[...]
    o_ref[...] = out


def _combine(x1, gates, yk, *, tc):
    t, d = x1.shape
    nt = t // tc
    ysp = lambda k: pl.BlockSpec((tc, d), lambda i: (k * nt + i, 0))
    return pl.pallas_call(
        _combine_kernel,
        out_shape=jax.ShapeDtypeStruct((t, d), F32),
        grid=(nt,),
        in_specs=[
            pl.BlockSpec((tc, d), lambda i: (i, 0)),
            pl.BlockSpec((tc, LANES), lambda i: (i, 0)),
            ysp(0), ysp(1), ysp(2), ysp(3),
        ],
        out_specs=pl.BlockSpec((tc, d), lambda i: (i, 0)),
        compiler_params=_cparams("parallel"),
        name="combine",
    )(x1, gates, yk, yk, yk, yk)


def _routing_plan(top_idx, n_experts, tm):
    t = top_idx.shape[0]
    tk = t * TOP_K
    nblk = tk // tm + n_experts
    flat_e = top_idx.reshape(tk)
    order = jnp.argsort(flat_e).astype(jnp.int32)
    experts = jnp.arange(n_experts, dtype=jnp.int32)
    counts = jnp.sum((flat_e[:, None] == experts[None, :]).astype(jnp.int32), axis=0)
    padded = ((counts + tm - 1) // tm) * tm
    start = jnp.cumsum(counts) - counts
    pcum = jnp.cumsum(padded)
    pstart = pcum - padded
    blk_start = jnp.arange(nblk, dtype=jnp.int32) * tm
    blk_expert = jnp.minimum(jnp.sum((pcum[None, :] <= blk_start[:, None]).astype(jnp.int32), axis=1),
                             n_experts - 1)
    rank = (blk_start - pstart[blk_expert])[:, None] + jnp.arange(tm, dtype=jnp.int32)[None, :]
    cnt = counts[blk_expert][:, None]
    valid = rank < cnt
    row_flat = order[jnp.minimum(start[blk_expert][:, None] + rank, tk - 1)]
    tok = row_flat // TOP_K
    kk = row_flat - tok * TOP_K
    src_tok = jnp.where(valid, tok, 0)
    dst_row = jnp.where(valid, kk * t + tok, tk + blk_expert[:, None] * tm + (rank - cnt))
    nact = (pcum[-1] // tm).astype(jnp.int32).reshape(1)
    return (blk_expert, nact, src_tok.reshape(nblk, 1, tm), dst_row.reshape(nblk, 1, tm), tk + n_experts * tm)


def kernel(x, mem, norm_mix_w, norm_mem_w, w_in, w_gate, b_gate, pool_w, pool_scale, q_norm_w, k_norm_w,
           lambda_q1, lambda_k1, lambda_q2, lambda_k2, diff_norm_w, rel_bias, w_mem_kv, mem_q_norm_w,
           mem_k_norm_w, w_up_pool, w_up_diff, w_up_mem, w_out, norm_ffn_w, router_w, router_b, w_gu, b_gu,
           w_down, b_down):
    batch, seq, d = x.shape
    mlen = mem.shape[1]
    t = batch * seq
    assert norm_mix_w.shape[0] == 1, "single-layer block"
    n_experts = router_w.shape[2]

    pool_width = len(POOL_WINDOWS) * POOL_GROUP_DIM
    qk_width = DIFF_HEADS * 2 * DIFF_QK_DIM
    v_width = DIFF_HEADS * DIFF_V_DIM
    mem_width = MEM_HEADS * MEM_HEAD_DIM
    q_col = pool_width
    k_col = q_col + qk_width
    v_col = k_col + qk_width
    mq_col = v_col + v_width
    chunk = 512
    assert w_in.shape[2] == mq_col + mem_width
    segs = (("p",) * (pool_width // chunk) + ("q",) * (qk_width // chunk) + ("k",) * (qk_width // chunk)
            + ("v",) * (v_width // chunk) + ("m",) * (mem_width // chunk))

    x2 = x.reshape(t, d)
    row = lambda a: a.reshape(1, -1)

    half = chunk // 2
    gmat = jnp.asarray(np.kron(np.eye(half // DIFF_QK_DIM), np.ones((DIFF_QK_DIM, DIFF_QK_DIM))), BF16)
    reps = chunk // DIFF_QK_DIM
    qw = row(jnp.tile(q_norm_w[0] * (1.0 / math.sqrt(DIFF_QK_DIM)), reps))
    kw = row(jnp.tile(k_norm_w[0], reps))
    mw = row(mem_q_norm_w[0] * (1.0 / math.sqrt(MEM_HEAD_DIM)))
    proj = _proj(x2, row(norm_mix_w[0]), w_in[0].astype(BF16), gmat, qw, kw, mw, tm=512, segs=segs, chunk=chunk)

    pool_out = _pool(proj, pool_w[0].astype(BF16), row(pool_scale[0]), batch=batch, seq=seq)

    rel_flat = rel_bias.reshape(-1)
    tables = _bias_tables(rel_flat, jnp.asarray(_bucket_tables(ATT_BLOCK)))
    lam_vecs = jnp.concatenate([lambda_q1, lambda_k1, lambda_q2, lambda_k2], axis=0)
    diff_out = _diff_attention(proj, tables, lam_vecs, row(diff_norm_w[0]),
                               batch=batch, seq=seq, q_col=q_col, k_col=k_col, v_col=v_col, hp=4)

    mk, mv = _mem_kv(mem.reshape(batch * mlen, d), row(norm_mem_w[0]), w_mem_kv[0].astype(BF16),
                     row(mem_k_norm_w[0]), batch=batch, mlen=mlen)
    mem_out = _mem_attention(proj, mk, mv, batch=batch, seq=seq, mlen=mlen, q_col=mq_col, tm=512)

    x1 = _merge(x2, row(norm_mix_w[0]), pool_out, diff_out, mem_out, w_gate[0].astype(BF16), row(b_gate[0]),
                w_up_pool[0].astype(BF16), w_up_diff[0].astype(BF16), w_up_mem[0].astype(BF16),
                w_out[0].astype(BF16), tm=512, tn=512)

    rw = router_w[0]
    rw_hi = rw.astype(BF16)
    rw_lo = (rw - rw_hi.astype(F32)).astype(BF16)
    idx_pad, gate_pad = _router(x1, row(norm_ffn_w[0]), rw_hi, rw_lo, row(router_b[0]), tm=512)
    tm_moe = 512
    blk_expert, nact, src_tok, dst_row, n_out_rows = _routing_plan(idx_pad[:, :TOP_K], n_experts, tm_moe)
    yk = _moe(blk_expert, nact, src_tok, dst_row, x1, row(norm_ffn_w[0]),
              w_gu[0], b_gu[0].reshape(n_experts, 1, -1),
              w_down[0], b_down[0].reshape(n_experts, 1, -1),
              tm=tm_moe, tf=512, n_out_rows=n_out_rows)
    out = _combine(x1, gate_pad, yk, tc=256)
    return out.reshape(batch, seq, d)
```

```python
import functools
import math

import numpy as np
import jax
import jax.numpy as jnp
from jax import lax
from jax.experimental import pallas as pl
from jax.experimental.pallas import tpu as pltpu

F32 = jnp.float32
BF16 = jnp.bfloat16

EPS = 1e-6
DIFF_HEADS = 8
DIFF_QK_DIM = 64
DIFF_V_DIM = 128
MEM_HEADS = 4
MEM_HEAD_DIM = 128
POOL_WINDOWS = (2, 4, 8, 16)
POOL_GROUP_DIM = 128
NUM_BUCKETS = 32
MAX_DISTANCE = 128
TOP_K = 4
SWIGLU_LIMIT = 7.0
SWIGLU_ALPHA = 1.702
LAM_INIT = 0.8 - 0.6 * math.exp(-0.3 * 0)

LANES = 128
VMEM_LIMIT = 56 * 1024 * 1024

ATT_BLOCK = 256
NT_DIMS = (((1,), (1,)), ((), ()))


def _cparams(*sem):
    return pltpu.CompilerParams(dimension_semantics=sem, vmem_limit_bytes=VMEM_LIMIT)


def _rms_rows(x, w):
    ms = jnp.mean(x * x, axis=-1, keepdims=True)
    return x * lax.rsqrt(ms + EPS) * w


def _proj_kernel(x_ref, nw_ref, w_ref, g_ref, qw_ref, kw_ref, mw_ref, o_ref, h_ref, *, chunk, segs):
    h_ref[...] = _rms_rows(x_ref[...], nw_ref[...]).astype(BF16)
    n = w_ref.shape[1]
    half = chunk // 2
    for c in range(n // chunk):
        c0 = c * chunk
        y = jnp.dot(h_ref[...], w_ref[:, c0:c0 + chunk], preferred_element_type=F32)
        kind = segs[c]
        if kind in ("q", "k"):
            wn = (qw_ref if kind == "q" else kw_ref)[...]
            outs = []
            for s in range(2):
                ys = y[:, s * half:(s + 1) * half]
                ss = jnp.dot((ys * ys).astype(BF16), g_ref[...], preferred_element_type=F32)
                outs.append(ys * lax.rsqrt(ss * (1.0 / DIFF_QK_DIM) + EPS) * wn[:, s * half:(s + 1) * half])
            y = jnp.concatenate(outs, axis=-1)
        elif kind == "m":
            outs = []
            for s in range(chunk // MEM_HEAD_DIM):
                ys = y[:, s * MEM_HEAD_DIM:(s + 1) * MEM_HEAD_DIM]
                outs.append(_rms_rows(ys, mw_ref[...]))
            y = jnp.concatenate(outs, axis=-1)
        o_ref[:, c0:c0 + chunk] = y.astype(BF16)


def _proj(x2, norm_w, w_in, gmat, qw, kw, mw, *, tm, segs, chunk):
    t, d = x2.shape
    n = w_in.shape[1]
    full = lambda shape: pl.BlockSpec(shape, lambda i: (0,) * len(shape))
    return pl.pallas_call(
        functools.partial(_proj_kernel, chunk=chunk, segs=segs),
        out_shape=jax.ShapeDtypeStruct((t, n), BF16),
        grid=(t // tm,),
        in_specs=[
            pl.BlockSpec((tm, d), lambda i: (i, 0)),
            full((1, d)),
            pl.BlockSpec((d, n), lambda i: (0, 0), pipeline_mode=pl.Buffered(1)),
            full(gmat.shape),
            full(qw.shape),
            full(kw.shape),
            full(mw.shape),
        ],
        out_specs=pl.BlockSpec((tm, n), lambda i: (i, 0)),
        scratch_shapes=[pltpu.VMEM((tm, d), BF16)],
        compiler_params=_cparams("parallel"),
        name="proj",
    )(x2, norm_w, w_in, gmat, qw, kw, mw)


def _pool_kernel(u_ref, pw_ref, ps_ref, o_ref):
    s_len = u_ref.shape[0]
    row = lax.broadcasted_iota(jnp.int32, (s_len, POOL_GROUP_DIM), 0)
    for g, w in enumerate(POOL_WINDOWS):
        cols = slice(g * POOL_GROUP_DIM, (g + 1) * POOL_GROUP_DIM)
        ug = u_ref[:, cols].astype(F32)
        acc = ug
        span = 1
        while span < w:
            shifted = pltpu.roll(acc, span, 0)
            acc = acc + jnp.where(row >= span, shifted, 0.0)
            span *= 2
        cnt = jnp.minimum(row + 1, w).astype(F32)
        pooled = acc / cnt - ug
        mixed = jnp.dot(pooled.astype(BF16), pw_ref[g], preferred_element_type=F32)
        o_ref[:, cols] = (mixed * ps_ref[:, cols]).astype(BF16)


def _pool(proj, pool_w, pool_scale, *, batch, seq):
    width = len(POOL_WINDOWS) * POOL_GROUP_DIM
    return pl.pallas_call(
        _pool_kernel,
        out_shape=jax.ShapeDtypeStruct((batch * seq, width), BF16),
        grid=(batch,),
        in_specs=[
            pl.BlockSpec((seq, width), lambda b: (b, 0)),
            pl.BlockSpec(pool_w.shape, lambda b: (0, 0, 0)),
            pl.BlockSpec((1, width), lambda b: (0, 0)),
        ],
        out_specs=pl.BlockSpec((seq, width), lambda b: (b, 0)),
        compiler_params=_cparams("parallel"),
        name="pool",
    )(proj, pool_w, pool_scale)


def _bucket_tables(blk):
    exact = NUM_BUCKETS // 2
    kj = np.arange(blk)[:, None]
    qi = np.arange(blk)[None, :]
    out = []
    for off in (0, 1):
        rel = qi - kj + off * blk
        n = np.maximum(rel, 0)
        nf = np.maximum(n, 1).astype(np.float64)
        large = exact + (np.log(nf / exact) / math.log(MAX_DISTANCE / exact) * (NUM_BUCKETS - exact)).astype(np.int64)
        large = np.minimum(large, NUM_BUCKETS - 1)
        bucket = np.where(n < exact, n, large)
        out.append(np.where(rel >= 0, bucket, -1))
    return np.stack(out).astype(np.int32)


def _bias_kernel(rb_ref, bucket_ref, o_ref):
    h = pl.program_id(0)
    far = rb_ref[(NUM_BUCKETS - 1) * DIFF_HEADS + h]
    for t in range(2):
        bk = bucket_ref[t]
        acc = jnp.full(bk.shape, -jnp.inf, F32)
        for b in range(NUM_BUCKETS):
            acc = jnp.where(bk == b, rb_ref[b * DIFF_HEADS + h] - far, acc)
        o_ref[0, t] = acc


def _bias_tables(rel_bias_flat, buckets):
    _, blk, _ = buckets.shape
    return pl.pallas_call(
        _bias_kernel,
        out_shape=jax.ShapeDtypeStruct((DIFF_HEADS, 2, blk, blk), F32),
        grid=(DIFF_HEADS,),
        in_specs=[
            pl.BlockSpec(memory_space=pltpu.SMEM),
            pl.BlockSpec(buckets.shape, lambda h: (0, 0, 0)),
        ],
        out_specs=pl.BlockSpec((1, 2, blk, blk), lambda h: (h, 0, 0, 0)),
        compiler_params=_cparams("arbitrary"),
        name="bias_tables",
    )(rel_bias_flat, buckets)


def _attn_kernel(q_ref, k_ref, v_ref, tbl_ref, lam_ref, nw_ref, o_ref, vt_ref, *, blk, hp):
    i = pl.program_id(2)
    nk = vt_ref.shape[1]
    heads = range(hp)
    hcols = lambda h: slice(h * LANES, (h + 1) * LANES)

    @pl.when(i == 0)
    def _():
        for h in heads:
            for c in range(nk):
                vt_ref[h, c] = v_ref[c * blk:(c + 1) * blk, hcols(h)].astype(F32).T.astype(BF16)

    lane = lax.broadcasted_iota(jnp.int32, (blk, LANES), 1)
    zero = jnp.zeros((blk, LANES), BF16)
    qq = []
    for h in heads:
        q = q_ref[:, hcols(h)]
        qq.append(jnp.concatenate([jnp.where(lane < DIFF_QK_DIM, q, zero),
                                   jnp.where(lane >= DIFF_QK_DIM, q, zero)], axis=0))

    def step(j, carry, biased):
        off = pl.multiple_of(j * blk, blk)
        scores = [lax.dot_general(k_ref[pl.ds(off, blk), hcols(h)], qq[h], NT_DIMS, preferred_element_type=F32)
                  for h in heads]
        stats, probs = [], []
        for h in heads:
            m, l, _ = carry[h]
            s = scores[h]
            if biased:
                bias = tbl_ref[h, i - j]
                s = s + jnp.concatenate([bias, bias], axis=1)
            m_new = jnp.maximum(m, jnp.max(s, axis=0, keepdims=True))
            p = jnp.exp(s - m_new)
            alpha = jnp.exp(m - m_new)
            stats.append((m_new, alpha * l + jnp.sum(p, axis=0, keepdims=True), alpha))
            probs.append(p.astype(BF16))
        pvs = [jnp.dot(vt_ref[h, j], probs[h], preferred_element_type=F32) for h in heads]
        return tuple((stats[h][0], stats[h][1], stats[h][2] * carry[h][2] + pvs[h]) for h in heads)

    init = tuple((jnp.full((1, 2 * blk), -jnp.inf, F32), jnp.zeros((1, 2 * blk), F32),
                  jnp.zeros((DIFF_V_DIM, 2 * blk), F32)) for _ in heads)
    carry = lax.fori_loop(0, i - 1, lambda j, c: step(j, c, False), init)
    carry = lax.fori_loop(jnp.maximum(i - 1, 0), i + 1, lambda j, c: step(j, c, True), carry)

    lv = lam_ref[...]
    lam = (jnp.exp(jnp.sum(lv[0:1] * lv[1:2], axis=-1, keepdims=True))
           - jnp.exp(jnp.sum(lv[2:3] * lv[3:4], axis=-1, keepdims=True)) + LAM_INIT)
    for h in heads:
        _, l, acc = carry[h]
        on = acc / l
        o = (on[:, :blk] - lam * on[:, blk:]).T
        o_ref[:, hcols(h)] = (_rms_rows(o, nw_ref[...]) * (1.0 - LAM_INIT)).astype(BF16)


def _diff_attention(proj, tables, lam_vecs, diff_norm_w, *, batch, seq, q_col, k_col, v_col, hp):
    blk = ATT_BLOCK
    nq = seq // blk
    width = hp * LANES
    qcb, kcb, vcb = q_col // width, k_col // width, v_col // width
    return pl.pallas_call(
        functools.partial(_attn_kernel, blk=blk, hp=hp),
        out_shape=jax.ShapeDtypeStruct((batch * seq, DIFF_HEADS * DIFF_V_DIM), BF16),
        grid=(batch, DIFF_HEADS // hp, nq),
        in_specs=[
            pl.BlockSpec((blk, width), lambda b, g, i: (b * nq + i, qcb + g)),
            pl.BlockSpec((seq, width), lambda b, g, i: (b, kcb + g)),
            pl.BlockSpec((seq, width), lambda b, g, i: (b, vcb + g)),
            pl.BlockSpec((hp, 2, blk, blk), lambda b, g, i: (g, 0, 0, 0)),
            pl.BlockSpec(lam_vecs.shape, lambda b, g, i: (0, 0)),
            pl.BlockSpec((1, DIFF_V_DIM), lambda b, g, i: (0, 0)),
        ],
        out_specs=pl.BlockSpec((blk, width), lambda b, g, i: (b * nq + i, g)),
        scratch_shapes=[pltpu.VMEM((hp, nq, DIFF_V_DIM, blk), BF16)],
        compiler_params=_cparams("arbitrary", "arbitrary", "arbitrary"),
        name="diff_attn",
    )(proj, proj, proj, tables, lam_vecs, diff_norm_w)


def _memkv_kernel(m_ref, nw_ref, w_ref, kw_ref, k_ref, v_ref):
    hm = _rms_rows(m_ref[...], nw_ref[...]).astype(BF16)
    kv = jnp.dot(hm, w_ref[...], preferred_element_type=F32)
    width = MEM_HEADS * MEM_HEAD_DIM
    for g in range(MEM_HEADS):
        cols = slice(g * MEM_HEAD_DIM, (g + 1) * MEM_HEAD_DIM)
        k_ref[:, cols] = _rms_rows(kv[:, cols], kw_ref[...]).astype(BF16)
    v_ref[...] = kv[:, width:].astype(BF16)


def _mem_kv(mem2, norm_w, w_kv, k_norm_w, *, batch, mlen):
    d = mem2.shape[1]
    width = MEM_HEADS * MEM_HEAD_DIM
    out = jax.ShapeDtypeStruct((batch * mlen, width), BF16)
    return pl.pallas_call(
        _memkv_kernel,
        out_shape=(out, out),
        grid=(batch,),
        in_specs=[
            pl.BlockSpec((mlen, d), lambda b: (b, 0)),
            pl.BlockSpec((1, d), lambda b: (0, 0)),
            pl.BlockSpec(w_kv.shape, lambda b: (0, 0)),
            pl.BlockSpec((1, MEM_HEAD_DIM), lambda b: (0, 0)),
        ],
        out_specs=(pl.BlockSpec((mlen, width), lambda b: (b, 0)),
                   pl.BlockSpec((mlen, width), lambda b: (b, 0))),
        compiler_params=_cparams("parallel"),
        name="mem_kv",
    )(mem2, norm_w, w_kv, k_norm_w)


def _memattn_kernel(q_ref, k_ref, v_ref, o_ref):
    for g in range(MEM_HEADS):
        cols = slice(g * MEM_HEAD_DIM, (g + 1) * MEM_HEAD_DIM)
        s = lax.dot_general(q_ref[:, cols], k_ref[:, cols], NT_DIMS, preferred_element_type=F32)
        s = s - jnp.max(s, axis=-1, keepdims=True)
        p = jnp.exp(s)
        p = p / jnp.sum(p, axis=-1, keepdims=True)
        o_ref[:, cols] = jnp.dot(p.astype(BF16), v_ref[:, cols], preferred_element_type=F32).astype(BF16)


def _mem_attention(proj, mk, mv, *, batch, seq, mlen, q_col, tm):
    width = MEM_HEADS * MEM_HEAD_DIM
    nt = seq // tm
    qcb = q_col // width
    return pl.pallas_call(
        _memattn_kernel,
        out_shape=jax.ShapeDtypeStruct((batch * seq, width), BF16),
        grid=(batch, nt),
        in_specs=[
            pl.BlockSpec((tm, width), lambda b, i: (b * nt + i, qcb)),
            pl.BlockSpec((mlen, width), lambda b, i: (b, 0)),
            pl.BlockSpec((mlen, width), lambda b, i: (b, 0)),
        ],
        out_specs=pl.BlockSpec((tm, width), lambda b, i: (b * nt + i, 0)),
        compiler_params=_cparams("parallel", "parallel"),
        name="mem_attn",
    )(proj, mk, mv)


def _merge_kernel(x_ref, nw_ref, pool_ref, diff_ref, mem_ref,
                  wg0_ref, wg1_ref, wg2_ref, bg0_ref, bg1_ref, bg2_ref,
                  wup_ref, wud_ref, wum_ref, wo_ref, o_ref, h_ref, acc_ref):
    n = pl.program_id(1)

    @pl.when(n == 0)
    def _():
        x = x_ref[...]
        h_ref[...] = _rms_rows(x, nw_ref[...]).astype(BF16)
        acc_ref[...] = x

    h = h_ref[...]

    def branch(a_ref, wg_ref, bg_ref, wu_ref):
        gate = jax.nn.sigmoid(jnp.dot(h, wg_ref[...], preferred_element_type=F32) + bg_ref[...])
        return gate * jnp.dot(a_ref[...], wu_ref[...], preferred_element_type=F32)

    y = (branch(pool_ref, wg0_ref, bg0_ref, wup_ref)
         + branch(diff_ref, wg1_ref, bg1_ref, wud_ref)
         + branch(mem_ref, wg2_ref, bg2_ref, wum_ref))
    acc_ref[...] += jnp.dot(y.astype(BF16), wo_ref[...], preferred_element_type=F32)

    @pl.when(n == pl.num_programs(1) - 1)
    def _():
        o_ref[...] = acc_ref[...]


def _merge(x2, norm_w, pool_out, diff_out, mem_out, w_gate, b_gate, w_up_pool, w_up_diff, w_up_mem, w_out,
           *, tm, tn):
    t, d = x2.shape
    nn = d // tn
    rows = lambda width: pl.BlockSpec((tm, width), lambda i, n: (i, 0))
    gate_w = lambda br: pl.BlockSpec((d, tn), lambda i, n: (0, br * nn + n))
    gate_b = lambda br: pl.BlockSpec((1, tn), lambda i, n: (0, br * nn + n))
    up_w = lambda width: pl.BlockSpec((width, tn), lambda i, n: (0, n))
    return pl.pallas_call(
        _merge_kernel,
        out_shape=jax.ShapeDtypeStruct((t, d), F32),
        grid=(t // tm, nn),
        in_specs=[
            rows(d),
            pl.BlockSpec((1, d), lambda i, n: (0, 0)),
            rows(pool_out.shape[1]), rows(diff_out.shape[1]), rows(mem_out.shape[1]),
            gate_w(0), gate_w(1), gate_w(2),
            gate_b(0), gate_b(1), gate_b(2),
            up_w(pool_out.shape[1]), up_w(diff_out.shape[1]), up_w(mem_out.shape[1]),
            pl.BlockSpec((tn, d), lambda i, n: (n, 0)),
        ],
        out_specs=pl.BlockSpec((tm, d), lambda i, n: (i, 0)),
        scratch_shapes=[pltpu.VMEM((tm, d), BF16), pltpu.VMEM((tm, d), F32)],
        compiler_params=_cparams("parallel", "arbitrary"),
        name="merge",
    )(x2, norm_w, pool_out, diff_out, mem_out, w_gate, w_gate, w_gate, b_gate, b_gate, b_gate,
      w_up_pool, w_up_diff, w_up_mem, w_out)


def _split_bf16(a):
    hi = a.astype(BF16)
    lo = (a - hi.astype(F32)).astype(BF16)
    return hi, lo


def _router_kernel(x_ref, nw_ref, whi_ref, wlo_ref, b_ref, idx_ref, gate_ref):
    h = _rms_rows(x_ref[...], nw_ref[...])
    hi, lo = _split_bf16(h)
    logits = (jnp.dot(hi, whi_ref[...], preferred_element_type=F32)
              + jnp.dot(hi, wlo_ref[...], preferred_element_type=F32)
              + jnp.dot(lo, whi_ref[...], preferred_element_type=F32)) + b_ref[...]
    tm, ne = logits.shape
    eid = lax.broadcasted_iota(jnp.int32, (tm, ne), 1)
    vals, ids = [], []
    cur = logits
    for _ in range(TOP_K):
        mx = jnp.max(cur, axis=-1, keepdims=True)
        sel = jnp.min(jnp.where(cur == mx, eid, ne), axis=-1, keepdims=True)
        vals.append(mx)
        ids.append(sel)
        cur = jnp.where(eid == sel, -jnp.inf, cur)
    exps = [jnp.exp(v - vals[0]) for v in vals]
    denom = exps[0] + exps[1] + exps[2] + exps[3]
    lane = lax.broadcasted_iota(jnp.int32, (tm, LANES), 1)
    idx_out = jnp.zeros((tm, LANES), jnp.int32)
    gate_out = jnp.zeros((tm, LANES), F32)
    for k in range(TOP_K):
        idx_out = jnp.where(lane == k, ids[k], idx_out)
        gate_out = jnp.where(lane == k, exps[k] / denom, gate_out)
    idx_ref[...] = idx_out
    gate_ref[...] = gate_out


def _router(x1, norm_w, w_hi, w_lo, bias, *, tm):
    t, d = x1.shape
    ne = w_hi.shape[1]
    return pl.pallas_call(
        _router_kernel,
        out_shape=(jax.ShapeDtypeStruct((t, LANES), jnp.int32), jax.ShapeDtypeStruct((t, LANES), F32)),
        grid=(t // tm,),
        in_specs=[
            pl.BlockSpec((tm, d), lambda i: (i, 0)),
            pl.BlockSpec((1, d), lambda i: (0, 0)),
            pl.BlockSpec((d, ne), lambda i: (0, 0)),
            pl.BlockSpec((d, ne), lambda i: (0, 0)),
            pl.BlockSpec((1, ne), lambda i: (0, 0)),
        ],
        out_specs=(pl.BlockSpec((tm, LANES), lambda i: (i, 0)),
                   pl.BlockSpec((tm, LANES), lambda i: (i, 0))),
        compiler_params=_cparams("parallel"),
        name="router",
    )(x1, norm_w, w_hi, w_lo, bias)


def _moe_kernel(se_ref, ba_ref, ns_ref, tok_ref, tokn_ref, dst_ref, x_hbm, nw_ref,
                wg_ref, wu_ref, bg_ref, bu_ref, wd_ref, bd_ref, y_hbm,
                xbuf, xn_ref, acc_ref, gsem, ssem, *, tm, nf):
    s = pl.program_id(0)
    f = pl.program_id(1)
    h = pl.program_id(2)
    nsup = pl.num_programs(0)
    k = 2 * s + h
    act_cur = ba_ref[k] > 0
    act_prev = (s >= 1) & (ba_ref[jnp.maximum(k - 2, 0)] > 0)
    act_next = (s + 1 < nsup) & (ba_ref[jnp.minimum(k + 2, 2 * nsup - 1)] > 0)

    def gather_row(idx_ref, hh, r):
        t = idx_ref[0, 0, r]
        pltpu.make_async_copy(x_hbm.at[pl.ds(t, 1)], xbuf.at[hh, pl.ds(r, 1)], gsem.at[hh]).start()

    def scatter_row(hh, r):
        t = dst_ref[0, 0, r]
        pltpu.make_async_copy(acc_ref.at[hh, pl.ds(r, 1)], y_hbm.at[pl.ds(t, 1)], ssem.at[hh]).start()

    def gather_wait(hh):
        pltpu.make_async_copy(x_hbm.at[pl.ds(0, tm)], xbuf.at[hh], gsem.at[hh]).wait()

    def scatter_wait(hh):
        pltpu.make_async_copy(acc_ref.at[hh], y_hbm.at[pl.ds(0, tm)], ssem.at[hh]).wait()

    for hh in (0, 1):
        mine = h == hh

        @pl.when(mine & (s == 0) & (f == 0) & act_cur)
        def _():
            def one(r, c):
                gather_row(tok_ref, hh, r)
                return c
            lax.fori_loop(0, tm, one, 0, unroll=8)

        @pl.when(mine & (f == 0) & act_prev)
        def _():
            scatter_wait(hh)

        @pl.when(mine & (f == 0) & act_cur)
        def _():
            gather_wait(hh)

        @pl.when(mine & (f == 1) & act_next)
        def _():
            for r in range(tm):
                gather_row(tokn_ref, hh, r)

    @pl.when(act_cur & (f == 0))
    def _():
        xn_ref[h] = _rms_rows(xbuf[h], nw_ref[...]).astype(BF16)

    @pl.when(act_cur)
    def _():
        xn = xn_ref[h]
        gate = jnp.dot(xn, wg_ref[0].astype(BF16), preferred_element_type=F32) + bg_ref[0]
        up = jnp.dot(xn, wu_ref[0].astype(BF16), preferred_element_type=F32) + bu_ref[0]
        gate = jnp.minimum(gate, SWIGLU_LIMIT)
        up = jnp.clip(up, -SWIGLU_LIMIT, SWIGLU_LIMIT)
        act = gate * jax.nn.sigmoid(gate * SWIGLU_ALPHA) * (up + 1.0)
        part = jnp.dot(act.astype(BF16), wd_ref[0].astype(BF16), preferred_element_type=F32)
        prev = acc_ref[h]
        acc_ref[h] = part + jnp.where(f == 0, jnp.broadcast_to(bd_ref[0], prev.shape), prev)

    for hh in (0, 1):
        @pl.when((h == hh) & (f == nf - 1) & act_cur)
        def _():
            for r in range(tm):
                scatter_row(hh, r)


def _moe(sup_expert, blk_active, nact, src_tok, dst_row, x1, norm_w, w_gu, b_gu, w_down, b_down,
         *, tm, tf, n_out_rows):
    t, d = x1.shape
    ne, _, two_f = w_gu.shape
    nf = two_f // 2 // tf
    assert nf >= 2, "the next pair's gather is issued in d_ff step 1"
    nblk = src_tok.shape[0]
    nsup = nblk // 2

    def eidx(s, se, ns):
        return se[jnp.minimum(s, ns[0] - 1)]

    def fidx(s, f, ns):
        return jnp.where(s < ns[0], f, nf - 1)

    idx_blk = lambda shift: pl.BlockSpec(
        (1, 1, tm), lambda s, f, h, se, ba, ns: (jnp.clip(2 * s + h + shift, 0, nblk - 1), 0, 0),
        memory_space=pltpu.SMEM)
    wspec = lambda shape, imap: pl.BlockSpec(shape, lambda s, f, h, se, ba, ns: imap(eidx(s, se, ns), fidx(s, f, ns)))
    grid_spec = pltpu.PrefetchScalarGridSpec(
        num_scalar_prefetch=3,
        grid=(nsup, nf, 2),
        in_specs=[
            idx_blk(0), idx_blk(2), idx_blk(0),
            pl.BlockSpec(memory_space=pl.ANY),
            pl.BlockSpec((1, d), lambda s, f, h, se, ba, ns: (0, 0)),
            wspec((1, d, tf), lambda e, f: (e, 0, f)),
            wspec((1, d, tf), lambda e, f: (e, 0, nf + f)),
            wspec((1, 1, tf), lambda e, f: (e, 0, f)),
            wspec((1, 1, tf), lambda e, f: (e, 0, nf + f)),
            wspec((1, tf, d), lambda e, f: (e, f, 0)),
            wspec((1, 1, d), lambda e, f: (e, 0, 0)),
        ],
        out_specs=pl.BlockSpec(memory_space=pl.ANY),
        scratch_shapes=[
            pltpu.VMEM((2, tm, d), F32),
            pltpu.VMEM((2, tm, d), BF16),
            pltpu.VMEM((2, tm, d), F32),
            pltpu.SemaphoreType.DMA((2,)),
            pltpu.SemaphoreType.DMA((2,)),
        ],
    )
    return pl.pallas_call(
        functools.partial(_moe_kernel, tm=tm, nf=nf),
        out_shape=jax.ShapeDtypeStruct((n_out_rows, d), F32),
        grid_spec=grid_spec,
        compiler_params=_cparams("arbitrary", "arbitrary", "arbitrary"),
        name="moe",
    )(sup_expert, blk_active, nact, src_tok, src_tok, dst_row, x1, norm_w, w_gu, w_gu, b_gu, b_gu, w_down, b_down)


def _combine_kernel(x_ref, g_ref, y0_ref, y1_ref, y2_ref, y3_ref, o_ref):
    g = g_ref[...]
    out = x_ref[...]
    for k, y_ref in enumerate((y0_ref, y1_ref, y2_ref, y3_ref)):
        out = out + g[:, k:k + 1] * y_ref[...]
    o_ref[...] = out


def _combine(x1, gates, yk, *, tc):
    t, d = x1.shape
    nt = t // tc
    ysp = lambda k: pl.BlockSpec((tc, d), lambda i: (k * nt + i, 0))
    return pl.pallas_call(
        _combine_kernel,
        out_shape=jax.ShapeDtypeStruct((t, d), F32),
        grid=(nt,),
        in_specs=[
            pl.BlockSpec((tc, d), lambda i: (i, 0)),
            pl.BlockSpec((tc, LANES), lambda i: (i, 0)),
            ysp(0), ysp(1), ysp(2), ysp(3),
        ],
        out_specs=pl.BlockSpec((tc, d), lambda i: (i, 0)),
        compiler_params=_cparams("parallel"),
        name="combine",
    )(x1, gates, yk, yk, yk, yk)


def _routing_plan(top_idx, n_experts, tm):
    t = top_idx.shape[0]
    tk = t * TOP_K
    grp = 2 * tm
    nsup = tk // grp + n_experts
    nblk = 2 * nsup
    flat_e = top_idx.reshape(tk)
    order = jnp.argsort(flat_e).astype(jnp.int32)
    experts = jnp.arange(n_experts, dtype=jnp.int32)
    counts = jnp.sum((flat_e[:, None] == experts[None, :]).astype(jnp.int32), axis=0)
    padded = ((counts + grp - 1) // grp) * grp
    start = jnp.cumsum(counts) - counts
    pcum = jnp.cumsum(padded)
    pstart = pcum - padded
    sup_start = jnp.arange(nsup, dtype=jnp.int32) * grp
    sup_expert = jnp.minimum(jnp.sum((pcum[None, :] <= sup_start[:, None]).astype(jnp.int32), axis=1),
                             n_experts - 1)
    blk_expert = jnp.repeat(sup_expert, 2)
    blk_start = jnp.arange(nblk, dtype=jnp.int32) * tm
    rank = (blk_start - pstart[blk_expert])[:, None] + jnp.arange(tm, dtype=jnp.int32)[None, :]
    cnt = counts[blk_expert][:, None]
    valid = rank < cnt
    blk_active = valid[:, 0].astype(jnp.int32)
    row_flat = order[jnp.minimum(start[blk_expert][:, None] + rank, tk - 1)]
    tok = row_flat // TOP_K
    kk = row_flat - tok * TOP_K
    src_tok = jnp.where(valid, tok, 0)
    dst_row = jnp.where(valid, kk * t + tok, tk + blk_expert[:, None] * grp + (rank - cnt))
    nact = (pcum[-1] // grp).astype(jnp.int32).reshape(1)
    return (sup_expert, blk_active, nact, src_tok.reshape(nblk, 1, tm), dst_row.reshape(nblk, 1, tm),
            tk + n_experts * grp)


def kernel(x, mem, norm_mix_w, norm_mem_w, w_in, w_gate, b_gate, pool_w, pool_scale, q_norm_w, k_norm_w,
           lambda_q1, lambda_k1, lambda_q2, lambda_k2, diff_norm_w, rel_bias, w_mem_kv, mem_q_norm_w,
           mem_k_norm_w, w_up_pool, w_up_diff, w_up_mem, w_out, norm_ffn_w, router_w, router_b, w_gu, b_gu,
           w_down, b_down):
    batch, seq, d = x.shape
    mlen = mem.shape[1]
    t = batch * seq
    assert norm_mix_w.shape[0] == 1, "single-layer block"
    n_experts = router_w.shape[2]

    pool_width = len(POOL_WINDOWS) * POOL_GROUP_DIM
    qk_width = DIFF_HEADS * 2 * DIFF_QK_DIM
    v_width = DIFF_HEADS * DIFF_V_DIM
    mem_width = MEM_HEADS * MEM_HEAD_DIM
    q_col = pool_width
    k_col = q_col + qk_width
    v_col = k_col + qk_width
    mq_col = v_col + v_width
    chunk = 512
    assert w_in.shape[2] == mq_col + mem_width
    segs = (("p",) * (pool_width // chunk) + ("q",) * (qk_width // chunk) + ("k",) * (qk_width // chunk)
            + ("v",) * (v_width // chunk) + ("m",) * (mem_width // chunk))

    x2 = x.reshape(t, d)
    row = lambda a: a.reshape(1, -1)

    half = chunk // 2
    gmat = jnp.asarray(np.kron(np.eye(half // DIFF_QK_DIM), np.ones((DIFF_QK_DIM, DIFF_QK_DIM))), BF16)
    reps = chunk // DIFF_QK_DIM
    qw = row(jnp.tile(q_norm_w[0] * (1.0 / math.sqrt(DIFF_QK_DIM)), reps))
    kw = row(jnp.tile(k_norm_w[0], reps))
    mw = row(mem_q_norm_w[0] * (1.0 / math.sqrt(MEM_HEAD_DIM)))
    proj = _proj(x2, row(norm_mix_w[0]), w_in[0].astype(BF16), gmat, qw, kw, mw, tm=512, segs=segs, chunk=chunk)

    pool_out = _pool(proj, pool_w[0].astype(BF16), row(pool_scale[0]), batch=batch, seq=seq)

    rel_flat = rel_bias.reshape(-1)
    tables = _bias_tables(rel_flat, jnp.asarray(_bucket_tables(ATT_BLOCK)))
    lam_vecs = jnp.concatenate([lambda_q1, lambda_k1, lambda_q2, lambda_k2], axis=0)
    diff_out = _diff_attention(proj, tables, lam_vecs, row(diff_norm_w[0]),
                               batch=batch, seq=seq, q_col=q_col, k_col=k_col, v_col=v_col, hp=4)

    mk, mv = _mem_kv(mem.reshape(batch * mlen, d), row(norm_mem_w[0]), w_mem_kv[0].astype(BF16),
                     row(mem_k_norm_w[0]), batch=batch, mlen=mlen)
    mem_out = _mem_attention(proj, mk, mv, batch=batch, seq=seq, mlen=mlen, q_col=mq_col, tm=512)

    x1 = _merge(x2, row(norm_mix_w[0]), pool_out, diff_out, mem_out, w_gate[0].astype(BF16), row(b_gate[0]),
                w_up_pool[0].astype(BF16), w_up_diff[0].astype(BF16), w_up_mem[0].astype(BF16),
                w_out[0].astype(BF16), tm=512, tn=512)

    rw = router_w[0]
    rw_hi = rw.astype(BF16)
    rw_lo = (rw - rw_hi.astype(F32)).astype(BF16)
    idx_pad, gate_pad = _router(x1, row(norm_ffn_w[0]), rw_hi, rw_lo, row(router_b[0]), tm=512)
    tm_moe = 512
    sup_expert, blk_active, nact, src_tok, dst_row, n_out_rows = _routing_plan(idx_pad[:, :TOP_K], n_experts, tm_moe)
    yk = _moe(sup_expert, blk_active, nact, src_tok, dst_row, x1, row(norm_ffn_w[0]),
              w_gu[0], b_gu[0].reshape(n_experts, 1, -1),
              w_down[0], b_down[0].reshape(n_experts, 1, -1),
              tm=tm_moe, tf=512, n_out_rows=n_out_rows)
    out = _combine(x1, gate_pad, yk, tc=256)
    return out.reshape(batch, seq, d)
```

```python
import functools
import math

import numpy as np
import jax
import jax.numpy as jnp
from jax import lax
from jax.experimental import pallas as pl
from jax.experimental.pallas import tpu as pltpu

F32 = jnp.float32
BF16 = jnp.bfloat16

EPS = 1e-6
DIFF_HEADS = 8
DIFF_QK_DIM = 64
DIFF_V_DIM = 128
MEM_HEADS = 4
MEM_HEAD_DIM = 128
POOL_WINDOWS = (2, 4, 8, 16)
POOL_GROUP_DIM = 128
NUM_BUCKETS = 32
MAX_DISTANCE = 128
TOP_K = 4
SWIGLU_LIMIT = 7.0
SWIGLU_ALPHA = 1.702
LAM_INIT = 0.8 - 0.6 * math.exp(-0.3 * 0)

LANES = 128
VMEM_LIMIT = 56 * 1024 * 1024

ATT_BLOCK = 512
NT_DIMS = (((1,), (1,)), ((), ()))


def _cparams(*sem):
    return pltpu.CompilerParams(dimension_semantics=sem, vmem_limit_bytes=VMEM_LIMIT)


def _rms_rows(x, w):
    ms = jnp.mean(x * x, axis=-1, keepdims=True)
    return x * lax.rsqrt(ms + EPS) * w


def _proj_kernel(x_ref, nw_ref, w_ref, g_ref, qw_ref, kw_ref, mw_ref, o_ref, h_ref, *, chunk, segs):
    h_ref[...] = _rms_rows(x_ref[...], nw_ref[...]).astype(BF16)
    n = w_ref.shape[1]
    half = chunk // 2
    for c in range(n // chunk):
        c0 = c * chunk
        y = jnp.dot(h_ref[...], w_ref[:, c0:c0 + chunk], preferred_element_type=F32)
        kind = segs[c]
        if kind in ("q", "k"):
            wn = (qw_ref if kind == "q" else kw_ref)[...]
            outs = []
            for s in range(2):
                ys = y[:, s * half:(s + 1) * half]
                ss = jnp.dot((ys * ys).astype(BF16), g_ref[...], preferred_element_type=F32)
                outs.append(ys * lax.rsqrt(ss * (1.0 / DIFF_QK_DIM) + EPS) * wn[:, s * half:(s + 1) * half])
            y = jnp.concatenate(outs, axis=-1)
        elif kind == "m":
            outs = []
            for s in range(chunk // MEM_HEAD_DIM):
                ys = y[:, s * MEM_HEAD_DIM:(s + 1) * MEM_HEAD_DIM]
                outs.append(_rms_rows(ys, mw_ref[...]))
            y = jnp.concatenate(outs, axis=-1)
        o_ref[:, c0:c0 + chunk] = y.astype(BF16)


def _proj(x2, norm_w, w_in, gmat, qw, kw, mw, *, tm, segs, chunk):
    t, d = x2.shape
    n = w_in.shape[1]
    full = lambda shape: pl.BlockSpec(shape, lambda i: (0,) * len(shape))
    return pl.pallas_call(
        functools.partial(_proj_kernel, chunk=chunk, segs=segs),
        out_shape=jax.ShapeDtypeStruct((t, n), BF16),
        grid=(t // tm,),
        in_specs=[
            pl.BlockSpec((tm, d), lambda i: (i, 0)),
            full((1, d)),
            pl.BlockSpec((d, n), lambda i: (0, 0), pipeline_mode=pl.Buffered(1)),
            full(gmat.shape),
            full(qw.shape),
            full(kw.shape),
            full(mw.shape),
        ],
        out_specs=pl.BlockSpec((tm, n), lambda i: (i, 0)),
        scratch_shapes=[pltpu.VMEM((tm, d), BF16)],
        compiler_params=_cparams("parallel"),
        name="proj",
    )(x2, norm_w, w_in, gmat, qw, kw, mw)


def _pool_kernel(u_ref, pw_ref, ps_ref, o_ref):
    s_len = u_ref.shape[0]
    row = lax.broadcasted_iota(jnp.int32, (s_len, POOL_GROUP_DIM), 0)
    for g, w in enumerate(POOL_WINDOWS):
        cols = slice(g * POOL_GROUP_DIM, (g + 1) * POOL_GROUP_DIM)
        ug = u_ref[:, cols].astype(F32)
        acc = ug
        span = 1
        while span < w:
            shifted = pltpu.roll(acc, span, 0)
            acc = acc + jnp.where(row >= span, shifted, 0.0)
            span *= 2
        cnt = jnp.minimum(row + 1, w).astype(F32)
        pooled = acc / cnt - ug
        mixed = jnp.dot(pooled.astype(BF16), pw_ref[g], preferred_element_type=F32)
        o_ref[:, cols] = (mixed * ps_ref[:, cols]).astype(BF16)


def _pool(proj, pool_w, pool_scale, *, batch, seq):
    width = len(POOL_WINDOWS) * POOL_GROUP_DIM
    return pl.pallas_call(
        _pool_kernel,
        out_shape=jax.ShapeDtypeStruct((batch * seq, width), BF16),
        grid=(batch,),
        in_specs=[
            pl.BlockSpec((seq, width), lambda b: (b, 0)),
            pl.BlockSpec(pool_w.shape, lambda b: (0, 0, 0)),
            pl.BlockSpec((1, width), lambda b: (0, 0)),
        ],
        out_specs=pl.BlockSpec((seq, width), lambda b: (b, 0)),
        compiler_params=_cparams("parallel"),
        name="pool",
    )(proj, pool_w, pool_scale)


def _bucket_tables(blk):
    exact = NUM_BUCKETS // 2
    kj = np.arange(blk)[:, None]
    qi = np.arange(blk)[None, :]
    out = []
    for off in (0, 1):
        rel = qi - kj + off * blk
        n = np.maximum(rel, 0)
        nf = np.maximum(n, 1).astype(np.float64)
        large = exact + (np.log(nf / exact) / math.log(MAX_DISTANCE / exact) * (NUM_BUCKETS - exact)).astype(np.int64)
        large = np.minimum(large, NUM_BUCKETS - 1)
        bucket = np.where(n < exact, n, large)
        out.append(np.where(rel >= 0, bucket, -1))
    return np.stack(out).astype(np.int32)


def _bias_kernel(rb_ref, bucket_ref, o_ref):
    h = pl.program_id(0)
    far = rb_ref[(NUM_BUCKETS - 1) * DIFF_HEADS + h]
    for t in range(2):
        bk = bucket_ref[t]
        acc = jnp.full(bk.shape, -jnp.inf, F32)
        for b in range(NUM_BUCKETS):
            acc = jnp.where(bk == b, rb_ref[b * DIFF_HEADS + h] - far, acc)
        o_ref[0, t] = acc


def _bias_tables(rel_bias_flat, buckets):
    _, blk, _ = buckets.shape
    return pl.pallas_call(
        _bias_kernel,
        out_shape=jax.ShapeDtypeStruct((DIFF_HEADS, 2, blk, blk), F32),
        grid=(DIFF_HEADS,),
        in_specs=[
            pl.BlockSpec(memory_space=pltpu.SMEM),
            pl.BlockSpec(buckets.shape, lambda h: (0, 0, 0)),
        ],
        out_specs=pl.BlockSpec((1, 2, blk, blk), lambda h: (h, 0, 0, 0)),
        compiler_params=_cparams("arbitrary"),
        name="bias_tables",
    )(rel_bias_flat, buckets)


def _attn_kernel(q_ref, k_ref, v_ref, tbl_ref, lam_ref, nw_ref, o_ref, vt_ref, *, blk, hp):
    i = pl.program_id(2)
    nk = vt_ref.shape[1]
    heads = range(hp)
    hcols = lambda h: slice(h * LANES, (h + 1) * LANES)

    @pl.when(i == 0)
    def _():
        for h in heads:
            for c in range(nk):
                vt_ref[h, c] = v_ref[c * blk:(c + 1) * blk, hcols(h)].astype(F32).T.astype(BF16)

    lane = lax.broadcasted_iota(jnp.int32, (blk, LANES), 1)
    zero = jnp.zeros((blk, LANES), BF16)
    qq = []
    for h in heads:
        q = q_ref[:, hcols(h)]
        qq.append(jnp.concatenate([jnp.where(lane < DIFF_QK_DIM, q, zero),
                                   jnp.where(lane >= DIFF_QK_DIM, q, zero)], axis=0))

    def step(j, carry, biased):
        off = pl.multiple_of(j * blk, blk)
        scores = [lax.dot_general(k_ref[pl.ds(off, blk), hcols(h)], qq[h], NT_DIMS, preferred_element_type=F32)
                  for h in heads]
        stats, probs = [], []
        for h in heads:
            m, l, _ = carry[h]
            s = scores[h]
            if biased:
                bias = tbl_ref[h, i - j]
                s = s + jnp.concatenate([bias, bias], axis=1)
            m_new = jnp.maximum(m, jnp.max(s, axis=0, keepdims=True))
            p = jnp.exp(s - m_new)
            alpha = jnp.exp(m - m_new)
            stats.append((m_new, alpha * l + jnp.sum(p, axis=0, keepdims=True), alpha))
            probs.append(p.astype(BF16))
        pvs = [jnp.dot(vt_ref[h, j], probs[h], preferred_element_type=F32) for h in heads]
        return tuple((stats[h][0], stats[h][1], stats[h][2] * carry[h][2] + pvs[h]) for h in heads)

    init = tuple((jnp.full((1, 2 * blk), -jnp.inf, F32), jnp.zeros((1, 2 * blk), F32),
                  jnp.zeros((DIFF_V_DIM, 2 * blk), F32)) for _ in heads)
    carry = lax.fori_loop(0, i - 1, lambda j, c: step(j, c, False), init)
    carry = lax.fori_loop(jnp.maximum(i - 1, 0), i + 1, lambda j, c: step(j, c, True), carry)

    lv = lam_ref[...]
    lam = (jnp.exp(jnp.sum(lv[0:1] * lv[1:2], axis=-1, keepdims=True))
           - jnp.exp(jnp.sum(lv[2:3] * lv[3:4], axis=-1, keepdims=True)) + LAM_INIT)
    for h in heads:
        _, l, acc = carry[h]
        on = acc / l
        o = (on[:, :blk] - lam * on[:, blk:]).T
        o_ref[:, hcols(h)] = (_rms_rows(o, nw_ref[...]) * (1.0 - LAM_INIT)).astype(BF16)


def _diff_attention(proj, tables, lam_vecs, diff_norm_w, *, batch, seq, q_col, k_col, v_col, hp):
    blk = ATT_BLOCK
    nq = seq // blk
    width = hp * LANES
    qcb, kcb, vcb = q_col // width, k_col // width, v_col // width
    return pl.pallas_call(
        functools.partial(_attn_kernel, blk=blk, hp=hp),
        out_shape=jax.ShapeDtypeStruct((batch * seq, DIFF_HEADS * DIFF_V_DIM), BF16),
        grid=(batch, DIFF_HEADS // hp, nq),
        in_specs=[
            pl.BlockSpec((blk, width), lambda b, g, i: (b * nq + i, qcb + g)),
            pl.BlockSpec((seq, width), lambda b, g, i: (b, kcb + g)),
            pl.BlockSpec((seq, width), lambda b, g, i: (b, vcb + g)),
            pl.BlockSpec((hp, 2, blk, blk), lambda b, g, i: (g, 0, 0, 0)),
            pl.BlockSpec(lam_vecs.shape, lambda b, g, i: (0, 0)),
            pl.BlockSpec((1, DIFF_V_DIM), lambda b, g, i: (0, 0)),
        ],
        out_specs=pl.BlockSpec((blk, width), lambda b, g, i: (b * nq + i, g)),
        scratch_shapes=[pltpu.VMEM((hp, nq, DIFF_V_DIM, blk), BF16)],
        compiler_params=_cparams("arbitrary", "arbitrary", "arbitrary"),
        name="diff_attn",
    )(proj, proj, proj, tables, lam_vecs, diff_norm_w)


def _memkv_kernel(m_ref, nw_ref, w_ref, kw_ref, k_ref, v_ref):
    hm = _rms_rows(m_ref[...], nw_ref[...]).astype(BF16)
    kv = jnp.dot(hm, w_ref[...], preferred_element_type=F32)
    width = MEM_HEADS * MEM_HEAD_DIM
    for g in range(MEM_HEADS):
        cols = slice(g * MEM_HEAD_DIM, (g + 1) * MEM_HEAD_DIM)
        k_ref[:, cols] = _rms_rows(kv[:, cols], kw_ref[...]).astype(BF16)
    v_ref[...] = kv[:, width:].astype(BF16)


def _mem_kv(mem2, norm_w, w_kv, k_norm_w, *, batch, mlen):
    d = mem2.shape[1]
    width = MEM_HEADS * MEM_HEAD_DIM
    out = jax.ShapeDtypeStruct((batch * mlen, width), BF16)
    return pl.pallas_call(
        _memkv_kernel,
        out_shape=(out, out),
        grid=(batch,),
        in_specs=[
            pl.BlockSpec((mlen, d), lambda b: (b, 0)),
            pl.BlockSpec((1, d), lambda b: (0, 0)),
            pl.BlockSpec(w_kv.shape, lambda b: (0, 0)),
            pl.BlockSpec((1, MEM_HEAD_DIM), lambda b: (0, 0)),
        ],
        out_specs=(pl.BlockSpec((mlen, width), lambda b: (b, 0)),
                   pl.BlockSpec((mlen, width), lambda b: (b, 0))),
        compiler_params=_cparams("parallel"),
        name="mem_kv",
    )(mem2, norm_w, w_kv, k_norm_w)


def _memattn_kernel(q_ref, k_ref, v_ref, o_ref):
    for g in range(MEM_HEADS):
        cols = slice(g * MEM_HEAD_DIM, (g + 1) * MEM_HEAD_DIM)
        s = lax.dot_general(q_ref[:, cols], k_ref[:, cols], NT_DIMS, preferred_element_type=F32)
        s = s - jnp.max(s, axis=-1, keepdims=True)
        p = jnp.exp(s)
        p = p / jnp.sum(p, axis=-1, keepdims=True)
        o_ref[:, cols] = jnp.dot(p.astype(BF16), v_ref[:, cols], preferred_element_type=F32).astype(BF16)


def _mem_attention(proj, mk, mv, *, batch, seq, mlen, q_col, tm):
    width = MEM_HEADS * MEM_HEAD_DIM
    nt = seq // tm
    qcb = q_col // width
    return pl.pallas_call(
        _memattn_kernel,
        out_shape=jax.ShapeDtypeStruct((batch * seq, width), BF16),
        grid=(batch, nt),
        in_specs=[
            pl.BlockSpec((tm, width), lambda b, i: (b * nt + i, qcb)),
            pl.BlockSpec((mlen, width), lambda b, i: (b, 0)),
            pl.BlockSpec((mlen, width), lambda b, i: (b, 0)),
        ],
        out_specs=pl.BlockSpec((tm, width), lambda b, i: (b * nt + i, 0)),
        compiler_params=_cparams("parallel", "parallel"),
        name="mem_attn",
    )(proj, mk, mv)


def _merge_kernel(x_ref, nw_ref, pool_ref, diff_ref, mem_ref,
                  wg0_ref, wg1_ref, wg2_ref, bg0_ref, bg1_ref, bg2_ref,
                  wup_ref, wud_ref, wum_ref, wo_ref, o_ref, h_ref, acc_ref):
    n = pl.program_id(1)

    @pl.when(n == 0)
    def _():
        x = x_ref[...]
        h_ref[...] = _rms_rows(x, nw_ref[...]).astype(BF16)
        acc_ref[...] = x

    h = h_ref[...]

    def branch(a_ref, wg_ref, bg_ref, wu_ref):
        gate = jax.nn.sigmoid(jnp.dot(h, wg_ref[...], preferred_element_type=F32) + bg_ref[...])
        return gate * jnp.dot(a_ref[...], wu_ref[...], preferred_element_type=F32)

    y = (branch(pool_ref, wg0_ref, bg0_ref, wup_ref)
         + branch(diff_ref, wg1_ref, bg1_ref, wud_ref)
         + branch(mem_ref, wg2_ref, bg2_ref, wum_ref))
    acc_ref[...] += jnp.dot(y.astype(BF16), wo_ref[...], preferred_element_type=F32)

    @pl.when(n == pl.num_programs(1) - 1)
    def _():
        o_ref[...] = acc_ref[...]


def _merge(x2, norm_w, pool_out, diff_out, mem_out, w_gate, b_gate, w_up_pool, w_up_diff, w_up_mem, w_out,
           *, tm, tn):
    t, d = x2.shape
    nn = d // tn
    rows = lambda width: pl.BlockSpec((tm, width), lambda i, n: (i, 0))
    gate_w = lambda br: pl.BlockSpec((d, tn), lambda i, n: (0, br * nn + n))
    gate_b = lambda br: pl.BlockSpec((1, tn), lambda i, n: (0, br * nn + n))
    up_w = lambda width: pl.BlockSpec((width, tn), lambda i, n: (0, n))
    return pl.pallas_call(
        _merge_kernel,
        out_shape=jax.ShapeDtypeStruct((t, d), F32),
        grid=(t // tm, nn),
        in_specs=[
            rows(d),
            pl.BlockSpec((1, d), lambda i, n: (0, 0)),
            rows(pool_out.shape[1]), rows(diff_out.shape[1]), rows(mem_out.shape[1]),
            gate_w(0), gate_w(1), gate_w(2),
            gate_b(0), gate_b(1), gate_b(2),
            up_w(pool_out.shape[1]), up_w(diff_out.shape[1]), up_w(mem_out.shape[1]),
            pl.BlockSpec((tn, d), lambda i, n: (n, 0)),
        ],
        out_specs=pl.BlockSpec((tm, d), lambda i, n: (i, 0)),
        scratch_shapes=[pltpu.VMEM((tm, d), BF16), pltpu.VMEM((tm, d), F32)],
        compiler_params=_cparams("parallel", "arbitrary"),
        name="merge",
    )(x2, norm_w, pool_out, diff_out, mem_out, w_gate, w_gate, w_gate, b_gate, b_gate, b_gate,
      w_up_pool, w_up_diff, w_up_mem, w_out)


def _split_bf16(a):
    hi = a.astype(BF16)
    lo = (a - hi.astype(F32)).astype(BF16)
    return hi, lo


def _router_kernel(x_ref, nw_ref, whi_ref, wlo_ref, b_ref, idx_ref, gate_ref):
    h = _rms_rows(x_ref[...], nw_ref[...])
    hi, lo = _split_bf16(h)
    logits = (jnp.dot(hi, whi_ref[...], preferred_element_type=F32)
              + jnp.dot(hi, wlo_ref[...], preferred_element_type=F32)
              + jnp.dot(lo, whi_ref[...], preferred_element_type=F32)) + b_ref[...]
    tm, ne = logits.shape
    eid = lax.broadcasted_iota(jnp.int32, (tm, ne), 1)
    vals, ids = [], []
    cur = logits
    for _ in range(TOP_K):
        mx = jnp.max(cur, axis=-1, keepdims=True)
        sel = jnp.min(jnp.where(cur == mx, eid, ne), axis=-1, keepdims=True)
        vals.append(mx)
        ids.append(sel)
        cur = jnp.where(eid == sel, -jnp.inf, cur)
    exps = [jnp.exp(v - vals[0]) for v in vals]
    denom = exps[0] + exps[1] + exps[2] + exps[3]
    lane = lax.broadcasted_iota(jnp.int32, (tm, LANES), 1)
    idx_out = jnp.zeros((tm, LANES), jnp.int32)
    gate_out = jnp.zeros((tm, LANES), F32)
    for k in range(TOP_K):
        idx_out = jnp.where(lane == k, ids[k], idx_out)
        gate_out = jnp.where(lane == k, exps[k] / denom, gate_out)
    idx_ref[...] = idx_out
    gate_ref[...] = gate_out


def _router(x1, norm_w, w_hi, w_lo, bias, *, tm):
    t, d = x1.shape
    ne = w_hi.shape[1]
    return pl.pallas_call(
        _router_kernel,
        out_shape=(jax.ShapeDtypeStruct((t, LANES), jnp.int32), jax.ShapeDtypeStruct((t, LANES), F32)),
        grid=(t // tm,),
        in_specs=[
            pl.BlockSpec((tm, d), lambda i: (i, 0)),
            pl.BlockSpec((1, d), lambda i: (0, 0)),
            pl.BlockSpec((d, ne), lambda i: (0, 0)),
            pl.BlockSpec((d, ne), lambda i: (0, 0)),
            pl.BlockSpec((1, ne), lambda i: (0, 0)),
        ],
        out_specs=(pl.BlockSpec((tm, LANES), lambda i: (i, 0)),
                   pl.BlockSpec((tm, LANES), lambda i: (i, 0))),
        compiler_params=_cparams("parallel"),
        name="router",
    )(x1, norm_w, w_hi, w_lo, bias)


def _moe_kernel(be_ref, na_ref, tok_ref, tokn_ref, dstp_ref, x_hbm, nw_ref,
                wg_ref, wu_ref, bg_ref, bu_ref, wd_ref, bd_ref, y_hbm,
                xbuf, xn_ref, acc_ref, ybuf, gsem, ssem, *, tm, nf):
    b = pl.program_id(0)
    f = pl.program_id(1)
    nb = pl.num_programs(0)
    nact = na_ref[0]
    slot = lax.rem(b, 2)
    other = 1 - slot

    def gather_row(idx_ref, s, r, prio=0):
        t = idx_ref[0, 0, r]
        pltpu.make_async_copy(x_hbm.at[pl.ds(t, 1)], xbuf.at[s, pl.ds(r, 1)], gsem.at[s]).start(priority=prio)

    def scatter_row(s, r, prio=0):
        t = dstp_ref[0, 0, r]
        pltpu.make_async_copy(ybuf.at[s, pl.ds(r, 1)], y_hbm.at[pl.ds(t, 1)], ssem.at[s]).start(priority=prio)

    def gather_wait(s):
        pltpu.make_async_copy(x_hbm.at[pl.ds(0, tm)], xbuf.at[s], gsem.at[s]).wait()

    def scatter_wait(s):
        pltpu.make_async_copy(ybuf.at[s], y_hbm.at[pl.ds(0, tm)], ssem.at[s]).wait()

    def all_rows(fn):
        def one(r, c):
            fn(r)
            return c
        lax.fori_loop(0, tm, one, 0, unroll=8)

    active = b < nact

    @pl.when((b == 0) & (f == 0))
    def _():
        all_rows(lambda r: gather_row(tok_ref, 0, r))

    @pl.when((b == nact) & (f == 0))
    def _():
        all_rows(lambda r: scatter_row(other, r))

    @pl.when(active & (f == 0))
    def _():
        gather_wait(slot)
        xn_ref[...] = _rms_rows(xbuf[slot], nw_ref[...]).astype(BF16)

    for s in (0, 1):
        @pl.when(active & (f == 1) & (b + 1 < nact) & (other == s))
        def _():
            for r in range(tm):
                gather_row(tokn_ref, s, r, r % 2)

        @pl.when(active & (f == 2) & (b >= 1) & (other == s))
        def _():
            for r in range(tm):
                scatter_row(s, r, r % 2)

    @pl.when(active)
    def _():
        xn = xn_ref[...]
        gate = jnp.dot(xn, wg_ref[0].astype(BF16), preferred_element_type=F32) + bg_ref[0]
        up = jnp.dot(xn, wu_ref[0].astype(BF16), preferred_element_type=F32) + bu_ref[0]
        gate = jnp.minimum(gate, SWIGLU_LIMIT)
        up = jnp.clip(up, -SWIGLU_LIMIT, SWIGLU_LIMIT)
        act = gate * jax.nn.sigmoid(gate * SWIGLU_ALPHA) * (up + 1.0)
        part = jnp.dot(act.astype(BF16), wd_ref[0].astype(BF16), preferred_element_type=F32)
        acc_ref[...] = jnp.where(f == 0, part, acc_ref[...] + part)

    @pl.when(active & (f == nf - 1))
    def _():
        @pl.when(b >= 2)
        def _():
            scatter_wait(slot)

        ybuf[slot] = acc_ref[...] + bd_ref[0]

    @pl.when((b == nb - 1) & (f == nf - 1))
    def _():
        scatter_wait(lax.rem(nact - 1, 2))

        @pl.when(nact >= 2)
        def _():
            scatter_wait(lax.rem(nact, 2))


def _moe(blk_expert, nact, src_tok, dst_row, x1, norm_w, w_gu, b_gu, w_down, b_down, *, tm, tf, n_out_rows):
    t, d = x1.shape
    ne, _, two_f = w_gu.shape
    nf = two_f // 2 // tf
    assert nf >= 3, "row DMAs are issued in d_ff steps 1 and 2"
    nblk = src_tok.shape[0]

    def eidx(b, be, na):
        return be[jnp.minimum(b, na[0] - 1)]

    def fidx(b, f, na):
        return jnp.where(b < na[0], f, nf - 1)

    idx_blk = lambda shift: pl.BlockSpec(
        (1, 1, tm), lambda b, f, be, na: (jnp.clip(b + shift, 0, nblk - 1), 0, 0), memory_space=pltpu.SMEM)
    grid_spec = pltpu.PrefetchScalarGridSpec(
        num_scalar_prefetch=2,
        grid=(nblk, nf),
        in_specs=[
            idx_blk(0), idx_blk(1), idx_blk(-1),
            pl.BlockSpec(memory_space=pl.ANY),
            pl.BlockSpec((1, d), lambda b, f, be, na: (0, 0)),
            pl.BlockSpec((1, d, tf), lambda b, f, be, na: (eidx(b, be, na), 0, fidx(b, f, na))),
            pl.BlockSpec((1, d, tf), lambda b, f, be, na: (eidx(b, be, na), 0, nf + fidx(b, f, na))),
            pl.BlockSpec((1, 1, tf), lambda b, f, be, na: (eidx(b, be, na), 0, fidx(b, f, na))),
            pl.BlockSpec((1, 1, tf), lambda b, f, be, na: (eidx(b, be, na), 0, nf + fidx(b, f, na))),
            pl.BlockSpec((1, tf, d), lambda b, f, be, na: (eidx(b, be, na), fidx(b, f, na), 0)),
            pl.BlockSpec((1, 1, d), lambda b, f, be, na: (eidx(b, be, na), 0, 0)),
        ],
        out_specs=pl.BlockSpec(memory_space=pl.ANY),
        scratch_shapes=[
            pltpu.VMEM((2, tm, d), F32),
            pltpu.VMEM((tm, d), BF16),
            pltpu.VMEM((tm, d), F32),
            pltpu.VMEM((2, tm, d), F32),
            pltpu.SemaphoreType.DMA((2,)),
            pltpu.SemaphoreType.DMA((2,)),
        ],
    )
    return pl.pallas_call(
        functools.partial(_moe_kernel, tm=tm, nf=nf),
        out_shape=jax.ShapeDtypeStruct((n_out_rows, d), F32),
        grid_spec=grid_spec,
        compiler_params=_cparams("arbitrary", "arbitrary"),
        name="moe",
    )(blk_expert, nact, src_tok, src_tok, dst_row, x1, norm_w, w_gu, w_gu, b_gu, b_gu, w_down, b_down)


def _combine_kernel(x_ref, g_ref, y0_ref, y1_ref, y2_ref, y3_ref, o_ref):
    g = g_ref[...]
    out = x_ref[...]
    for k, y_ref in enumerate((y0_ref, y1_ref, y2_ref, y3_ref)):
        out = out + g[:, k:k + 1] * y_ref[...]
    o_ref[...] = out


def _combine(x1, gates, yk, *, tc):
    t, d = x1.shape
    nt = t // tc
    ysp = lambda k: pl.BlockSpec((tc, d), lambda i: (k * nt + i, 0))
    return pl.pallas_call(
        _combine_kernel,
        out_shape=jax.ShapeDtypeStruct((t, d), F32),
        grid=(nt,),
        in_specs=[
            pl.BlockSpec((tc, d), lambda i: (i, 0)),
            pl.BlockSpec((tc, LANES), lambda i: (i, 0)),
            ysp(0), ysp(1), ysp(2), ysp(3),
        ],
        out_specs=pl.BlockSpec((tc, d), lambda i: (i, 0)),
        compiler_params=_cparams("parallel"),
        name="combine",
    )(x1, gates, yk, yk, yk, yk)


def _routing_plan(top_idx, n_experts, tm):
    t = top_idx.shape[0]
    tk = t * TOP_K
    nblk = tk // tm + n_experts
    flat_e = top_idx.reshape(tk)
    order = jnp.argsort(flat_e).astype(jnp.int32)
    experts = jnp.arange(n_experts, dtype=jnp.int32)
    counts = jnp.sum((flat_e[:, None] == experts[None, :]).astype(jnp.int32), axis=0)
    padded = ((counts + tm - 1) // tm) * tm
    start = jnp.cumsum(counts) - counts
    pcum = jnp.cumsum(padded)
    pstart = pcum - padded
    blk_start = jnp.arange(nblk, dtype=jnp.int32) * tm
    blk_expert = jnp.minimum(jnp.sum((pcum[None, :] <= blk_start[:, None]).astype(jnp.int32), axis=1),
                             n_experts - 1)
    rank = (blk_start - pstart[blk_expert])[:, None] + jnp.arange(tm, dtype=jnp.int32)[None, :]
    cnt = counts[blk_expert][:, None]
    valid = rank < cnt
    row_flat = order[jnp.minimum(start[blk_expert][:, None] + rank, tk - 1)]
    tok = row_flat // TOP_K
    kk = row_flat - tok * TOP_K
    src_tok = jnp.where(valid, tok, 0)
    dst_row = jnp.where(valid, kk * t + tok, tk + blk_expert[:, None] * tm + (rank - cnt))
    nact = (pcum[-1] // tm).astype(jnp.int32).reshape(1)
    return (blk_expert, nact, src_tok.reshape(nblk, 1, tm), dst_row.reshape(nblk, 1, tm), tk + n_experts * tm)


def kernel(x, mem, norm_mix_w, norm_mem_w, w_in, w_gate, b_gate, pool_w, pool_scale, q_norm_w, k_norm_w,
           lambda_q1, lambda_k1, lambda_q2, lambda_k2, diff_norm_w, rel_bias, w_mem_kv, mem_q_norm_w,
           mem_k_norm_w, w_up_pool, w_up_diff, w_up_mem, w_out, norm_ffn_w, router_w, router_b, w_gu, b_gu,
           w_down, b_down):
    batch, seq, d = x.shape
    mlen = mem.shape[1]
    t = batch * seq
    assert norm_mix_w.shape[0] == 1, "single-layer block"
    n_experts = router_w.shape[2]

    pool_width = len(POOL_WINDOWS) * POOL_GROUP_DIM
    qk_width = DIFF_HEADS * 2 * DIFF_QK_DIM
    v_width = DIFF_HEADS * DIFF_V_DIM
    mem_width = MEM_HEADS * MEM_HEAD_DIM
    q_col = pool_width
    k_col = q_col + qk_width
    v_col = k_col + qk_width
    mq_col = v_col + v_width
    chunk = 512
    assert w_in.shape[2] == mq_col + mem_width
    segs = (("p",) * (pool_width // chunk) + ("q",) * (qk_width // chunk) + ("k",) * (qk_width // chunk)
            + ("v",) * (v_width // chunk) + ("m",) * (mem_width // chunk))

    x2 = x.reshape(t, d)
    row = lambda a: a.reshape(1, -1)

    half = chunk // 2
    gmat = jnp.asarray(np.kron(np.eye(half // DIFF_QK_DIM), np.ones((DIFF_QK_DIM, DIFF_QK_DIM))), BF16)
    reps = chunk // DIFF_QK_DIM
    qw = row(jnp.tile(q_norm_w[0] * (1.0 / math.sqrt(DIFF_QK_DIM)), reps))
    kw = row(jnp.tile(k_norm_w[0], reps))
    mw = row(mem_q_norm_w[0] * (1.0 / math.sqrt(MEM_HEAD_DIM)))
    proj = _proj(x2, row(norm_mix_w[0]), w_in[0].astype(BF16), gmat, qw, kw, mw, tm=512, segs=segs, chunk=chunk)

    pool_out = _pool(proj, pool_w[0].astype(BF16), row(pool_scale[0]), batch=batch, seq=seq)

    rel_flat = rel_bias.reshape(-1)
    tables = _bias_tables(rel_flat, jnp.asarray(_bucket_tables(ATT_BLOCK)))
    lam_vecs = jnp.concatenate([lambda_q1, lambda_k1, lambda_q2, lambda_k2], axis=0)
    diff_out = _diff_attention(proj, tables, lam_vecs, row(diff_norm_w[0]),
                               batch=batch, seq=seq, q_col=q_col, k_col=k_col, v_col=v_col, hp=4)

    mk, mv = _mem_kv(mem.reshape(batch * mlen, d), row(norm_mem_w[0]), w_mem_kv[0].astype(BF16),
                     row(mem_k_norm_w[0]), batch=batch, mlen=mlen)
    mem_out = _mem_attention(proj, mk, mv, batch=batch, seq=seq, mlen=mlen, q_col=mq_col, tm=512)

    x1 = _merge(x2, row(norm_mix_w[0]), pool_out, diff_out, mem_out, w_gate[0].astype(BF16), row(b_gate[0]),
                w_up_pool[0].astype(BF16), w_up_diff[0].astype(BF16), w_up_mem[0].astype(BF16),
                w_out[0].astype(BF16), tm=512, tn=512)

    rw = router_w[0]
    rw_hi = rw.astype(BF16)
    rw_lo = (rw - rw_hi.astype(F32)).astype(BF16)
    idx_pad, gate_pad = _router(x1, row(norm_ffn_w[0]), rw_hi, rw_lo, row(router_b[0]), tm=512)
    tm_moe = 512
    blk_expert, nact, src_tok, dst_row, n_out_rows = _routing_plan(idx_pad[:, :TOP_K], n_experts, tm_moe)
    yk = _moe(blk_expert, nact, src_tok, dst_row, x1, row(norm_ffn_w[0]),
              w_gu[0], b_gu[0].reshape(n_experts, 1, -1),
              w_down[0], b_down[0].reshape(n_experts, 1, -1),
              tm=tm_moe, tf=512, n_out_rows=n_out_rows)
    out = _combine(x1, gate_pad, yk, tc=256)
    return out.reshape(batch, seq, d)
```

```python
import functools
import math

import numpy as np
import jax
import jax.numpy as jnp
from jax import lax
from jax.experimental import pallas as pl
from jax.experimental.pallas import tpu as pltpu

F32 = jnp.float32
BF16 = jnp.bfloat16

EPS = 1e-6
DIFF_HEADS = 8
DIFF_QK_DIM = 64
DIFF_V_DIM = 128
MEM_HEADS = 4
MEM_HEAD_DIM = 128
POOL_WINDOWS = (2, 4, 8, 16)
POOL_GROUP_DIM = 128
NUM_BUCKETS = 32
MAX_DISTANCE = 128
TOP_K = 4
SWIGLU_LIMIT = 7.0
SWIGLU_ALPHA = 1.702
LAM_INIT = 0.8 - 0.6 * math.exp(-0.3 * 0)

LANES = 128
VMEM_LIMIT = 56 * 1024 * 1024

ATT_BLOCK = 512
NT_DIMS = (((1,), (1,)), ((), ()))


def _cparams(*sem):
    return pltpu.CompilerParams(dimension_semantics=sem, vmem_limit_bytes=VMEM_LIMIT)


def _rms_rows(x, w):
    ms = jnp.mean(x * x, axis=-1, keepdims=True)
    return x * lax.rsqrt(ms + EPS) * w


def _proj_kernel(x_ref, nw_ref, w_ref, g_ref, qw_ref, kw_ref, mw_ref, o_ref, h_ref, *, chunk, segs):
    h_ref[...] = _rms_rows(x_ref[...], nw_ref[...]).astype(BF16)
    n = w_ref.shape[1]
    half = chunk // 2
    for c in range(n // chunk):
        c0 = c * chunk
        y = jnp.dot(h_ref[...], w_ref[:, c0:c0 + chunk], preferred_element_type=F32)
        kind = segs[c]
        if kind in ("q", "k"):
            wn = (qw_ref if kind == "q" else kw_ref)[...]
            outs = []
            for s in range(2):
                ys = y[:, s * half:(s + 1) * half]
                ss = jnp.dot((ys * ys).astype(BF16), g_ref[...], preferred_element_type=F32)
                outs.append(ys * lax.rsqrt(ss * (1.0 / DIFF_QK_DIM) + EPS) * wn[:, s * half:(s + 1) * half])
            y = jnp.concatenate(outs, axis=-1)
        elif kind == "m":
            outs = []
            for s in range(chunk // MEM_HEAD_DIM):
                ys = y[:, s * MEM_HEAD_DIM:(s + 1) * MEM_HEAD_DIM]
                outs.append(_rms_rows(ys, mw_ref[...]))
            y = jnp.concatenate(outs, axis=-1)
        o_ref[:, c0:c0 + chunk] = y.astype(BF16)


def _proj(x2, norm_w, w_in, gmat, qw, kw, mw, *, tm, segs, chunk):
    t, d = x2.shape
    n = w_in.shape[1]
    full = lambda shape: pl.BlockSpec(shape, lambda i: (0,) * len(shape))
    return pl.pallas_call(
        functools.partial(_proj_kernel, chunk=chunk, segs=segs),
        out_shape=jax.ShapeDtypeStruct((t, n), BF16),
        grid=(t // tm,),
        in_specs=[
            pl.BlockSpec((tm, d), lambda i: (i, 0)),
            full((1, d)),
            pl.BlockSpec((d, n), lambda i: (0, 0), pipeline_mode=pl.Buffered(1)),
            full(gmat.shape),
            full(qw.shape),
            full(kw.shape),
            full(mw.shape),
        ],
        out_specs=pl.BlockSpec((tm, n), lambda i: (i, 0)),
        scratch_shapes=[pltpu.VMEM((tm, d), BF16)],
        compiler_params=_cparams("parallel"),
        name="proj",
    )(x2, norm_w, w_in, gmat, qw, kw, mw)


def _pool_kernel(u_ref, pw_ref, ps_ref, o_ref):
    s_len = u_ref.shape[0]
    row = lax.broadcasted_iota(jnp.int32, (s_len, POOL_GROUP_DIM), 0)
    for g, w in enumerate(POOL_WINDOWS):
        cols = slice(g * POOL_GROUP_DIM, (g + 1) * POOL_GROUP_DIM)
        ug = u_ref[:, cols].astype(F32)
        acc = ug
        span = 1
        while span < w:
            shifted = pltpu.roll(acc, span, 0)
            acc = acc + jnp.where(row >= span, shifted, 0.0)
            span *= 2
        cnt = jnp.minimum(row + 1, w).astype(F32)
        pooled = acc / cnt - ug
        mixed = jnp.dot(pooled.astype(BF16), pw_ref[g], preferred_element_type=F32)
        o_ref[:, cols] = (mixed * ps_ref[:, cols]).astype(BF16)


def _pool(proj, pool_w, pool_scale, *, batch, seq):
    width = len(POOL_WINDOWS) * POOL_GROUP_DIM
    return pl.pallas_call(
        _pool_kernel,
        out_shape=jax.ShapeDtypeStruct((batch * seq, width), BF16),
        grid=(batch,),
        in_specs=[
            pl.BlockSpec((seq, width), lambda b: (b, 0)),
            pl.BlockSpec(pool_w.shape, lambda b: (0, 0, 0)),
            pl.BlockSpec((1, width), lambda b: (0, 0)),
        ],
        out_specs=pl.BlockSpec((seq, width), lambda b: (b, 0)),
        compiler_params=_cparams("parallel"),
        name="pool",
    )(proj, pool_w, pool_scale)


def _bucket_tables(blk):
    exact = NUM_BUCKETS // 2
    kj = np.arange(blk)[:, None]
    qi = np.arange(blk)[None, :]
    out = []
    for off in (0, 1):
        rel = qi - kj + off * blk
        n = np.maximum(rel, 0)
        nf = np.maximum(n, 1).astype(np.float64)
        large = exact + (np.log(nf / exact) / math.log(MAX_DISTANCE / exact) * (NUM_BUCKETS - exact)).astype(np.int64)
        large = np.minimum(large, NUM_BUCKETS - 1)
        bucket = np.where(n < exact, n, large)
        out.append(np.where(rel >= 0, bucket, -1))
    return np.stack(out).astype(np.int32)


def _bias_kernel(rb_ref, bucket_ref, o_ref):
    h = pl.program_id(0)
    far = rb_ref[(NUM_BUCKETS - 1) * DIFF_HEADS + h]
    for t in range(2):
        bk = bucket_ref[t]
        acc = jnp.full(bk.shape, -jnp.inf, F32)
        for b in range(NUM_BUCKETS):
            acc = jnp.where(bk == b, rb_ref[b * DIFF_HEADS + h] - far, acc)
        o_ref[0, t] = acc


def _bias_tables(rel_bias_flat, buckets):
    _, blk, _ = buckets.shape
    return pl.pallas_call(
        _bias_kernel,
        out_shape=jax.ShapeDtypeStruct((DIFF_HEADS, 2, blk, blk), F32),
        grid=(DIFF_HEADS,),
        in_specs=[
            pl.BlockSpec(memory_space=pltpu.SMEM),
            pl.BlockSpec(buckets.shape, lambda h: (0, 0, 0)),
        ],
        out_specs=pl.BlockSpec((1, 2, blk, blk), lambda h: (h, 0, 0, 0)),
        compiler_params=_cparams("arbitrary"),
        name="bias_tables",
    )(rel_bias_flat, buckets)


def _attn_kernel(q_ref, k_ref, v_ref, tbl_ref, lam_ref, nw_ref, o_ref, vt_ref, *, blk, hp):
    i = pl.program_id(2)
    nk = vt_ref.shape[1]
    heads = range(hp)
    hcols = lambda h: slice(h * LANES, (h + 1) * LANES)

    @pl.when(i == 0)
    def _():
        for h in heads:
            for c in range(nk):
                vt_ref[h, c] = v_ref[c * blk:(c + 1) * blk, hcols(h)].astype(F32).T.astype(BF16)

    lane = lax.broadcasted_iota(jnp.int32, (blk, LANES), 1)
    zero = jnp.zeros((blk, LANES), BF16)
    qq = []
    for h in heads:
        q = q_ref[:, hcols(h)]
        qq.append(jnp.concatenate([jnp.where(lane < DIFF_QK_DIM, q, zero),
                                   jnp.where(lane >= DIFF_QK_DIM, q, zero)], axis=0))

    def step(j, carry, biased):
        off = pl.multiple_of(j * blk, blk)
        scores = [lax.dot_general(k_ref[pl.ds(off, blk), hcols(h)], qq[h], NT_DIMS, preferred_element_type=F32)
                  for h in heads]
        stats, probs = [], []
        for h in heads:
            m, l, _ = carry[h]
            s = scores[h]
            if biased:
                bias = tbl_ref[h, i - j]
                s = s + jnp.concatenate([bias, bias], axis=1)
            m_new = jnp.maximum(m, jnp.max(s, axis=0, keepdims=True))
            p = jnp.exp(s - m_new)
            alpha = jnp.exp(m - m_new)
            stats.append((m_new, alpha * l + jnp.sum(p, axis=0, keepdims=True), alpha))
            probs.append(p.astype(BF16))
        pvs = [jnp.dot(vt_ref[h, j], probs[h], preferred_element_type=F32) for h in heads]
        return tuple((stats[h][0], stats[h][1], stats[h][2] * carry[h][2] + pvs[h]) for h in heads)

    init = tuple((jnp.full((1, 2 * blk), -jnp.inf, F32), jnp.zeros((1, 2 * blk), F32),
                  jnp.zeros((DIFF_V_DIM, 2 * blk), F32)) for _ in heads)
    carry = lax.fori_loop(0, i - 1, lambda j, c: step(j, c, False), init)
    carry = lax.fori_loop(jnp.maximum(i - 1, 0), i + 1, lambda j, c: step(j, c, True), carry)

    lv = lam_ref[...]
    lam = (jnp.exp(jnp.sum(lv[0:1] * lv[1:2], axis=-1, keepdims=True))
           - jnp.exp(jnp.sum(lv[2:3] * lv[3:4], axis=-1, keepdims=True)) + LAM_INIT)
    for h in heads:
        _, l, acc = carry[h]
        on = acc / l
        o = (on[:, :blk] - lam * on[:, blk:]).T
        o_ref[:, hcols(h)] = (_rms_rows(o, nw_ref[...]) * (1.0 - LAM_INIT)).astype(BF16)


def _diff_attention(proj, tables, lam_vecs, diff_norm_w, *, batch, seq, q_col, k_col, v_col, hp):
    blk = ATT_BLOCK
    nq = seq // blk
    width = hp * LANES
    qcb, kcb, vcb = q_col // width, k_col // width, v_col // width
    return pl.pallas_call(
        functools.partial(_attn_kernel, blk=blk, hp=hp),
        out_shape=jax.ShapeDtypeStruct((batch * seq, DIFF_HEADS * DIFF_V_DIM), BF16),
        grid=(batch, DIFF_HEADS // hp, nq),
        in_specs=[
            pl.BlockSpec((blk, width), lambda b, g, i: (b * nq + i, qcb + g)),
            pl.BlockSpec((seq, width), lambda b, g, i: (b, kcb + g)),
            pl.BlockSpec((seq, width), lambda b, g, i: (b, vcb + g)),
            pl.BlockSpec((hp, 2, blk, blk), lambda b, g, i: (g, 0, 0, 0)),
            pl.BlockSpec(lam_vecs.shape, lambda b, g, i: (0, 0)),
            pl.BlockSpec((1, DIFF_V_DIM), lambda b, g, i: (0, 0)),
        ],
        out_specs=pl.BlockSpec((blk, width), lambda b, g, i: (b * nq + i, g)),
        scratch_shapes=[pltpu.VMEM((hp, nq, DIFF_V_DIM, blk), BF16)],
        compiler_params=_cparams("arbitrary", "arbitrary", "arbitrary"),
        name="diff_attn",
    )(proj, proj, proj, tables, lam_vecs, diff_norm_w)


def _memkv_kernel(m_ref, nw_ref, w_ref, kw_ref, k_ref, v_ref):
    hm = _rms_rows(m_ref[...], nw_ref[...]).astype(BF16)
    kv = jnp.dot(hm, w_ref[...], preferred_element_type=F32)
    width = MEM_HEADS * MEM_HEAD_DIM
    for g in range(MEM_HEADS):
        cols = slice(g * MEM_HEAD_DIM, (g + 1) * MEM_HEAD_DIM)
        k_ref[:, cols] = _rms_rows(kv[:, cols], kw_ref[...]).astype(BF16)
    v_ref[...] = kv[:, width:].astype(BF16)


def _mem_kv(mem2, norm_w, w_kv, k_norm_w, *, batch, mlen):
    d = mem2.shape[1]
    width = MEM_HEADS * MEM_HEAD_DIM
    out = jax.ShapeDtypeStruct((batch * mlen, width), BF16)
    return pl.pallas_call(
        _memkv_kernel,
        out_shape=(out, out),
        grid=(batch,),
        in_specs=[
            pl.BlockSpec((mlen, d), lambda b: (b, 0)),
            pl.BlockSpec((1, d), lambda b: (0, 0)),
            pl.BlockSpec(w_kv.shape, lambda b: (0, 0)),
            pl.BlockSpec((1, MEM_HEAD_DIM), lambda b: (0, 0)),
        ],
        out_specs=(pl.BlockSpec((mlen, width), lambda b: (b, 0)),
                   pl.BlockSpec((mlen, width), lambda b: (b, 0))),
        compiler_params=_cparams("parallel"),
        name="mem_kv",
    )(mem2, norm_w, w_kv, k_norm_w)


def _memattn_kernel(q_ref, k_ref, v_ref, o_ref):
    for g in range(MEM_HEADS):
        cols = slice(g * MEM_HEAD_DIM, (g + 1) * MEM_HEAD_DIM)
        s = lax.dot_general(q_ref[:, cols], k_ref[:, cols], NT_DIMS, preferred_element_type=F32)
        s = s - jnp.max(s, axis=-1, keepdims=True)
        p = jnp.exp(s)
        p = p / jnp.sum(p, axis=-1, keepdims=True)
        o_ref[:, cols] = jnp.dot(p.astype(BF16), v_ref[:, cols], preferred_element_type=F32).astype(BF16)


def _mem_attention(proj, mk, mv, *, batch, seq, mlen, q_col, tm):
    width = MEM_HEADS * MEM_HEAD_DIM
    nt = seq // tm
    qcb = q_col // width
    return pl.pallas_call(
        _memattn_kernel,
        out_shape=jax.ShapeDtypeStruct((batch * seq, width), BF16),
        grid=(batch, nt),
        in_specs=[
            pl.BlockSpec((tm, width), lambda b, i: (b * nt + i, qcb)),
            pl.BlockSpec((mlen, width), lambda b, i: (b, 0)),
            pl.BlockSpec((mlen, width), lambda b, i: (b, 0)),
        ],
        out_specs=pl.BlockSpec((tm, width), lambda b, i: (b * nt + i, 0)),
        compiler_params=_cparams("parallel", "parallel"),
        name="mem_attn",
    )(proj, mk, mv)


def _merge_kernel(x_ref, nw_ref, pool_ref, diff_ref, mem_ref,
                  wg0_ref, wg1_ref, wg2_ref, bg0_ref, bg1_ref, bg2_ref,
                  wup_ref, wud_ref, wum_ref, wo_ref, o_ref, h_ref, acc_ref):
    n = pl.program_id(1)

    @pl.when(n == 0)
    def _():
        x = x_ref[...]
        h_ref[...] = _rms_rows(x, nw_ref[...]).astype(BF16)
        acc_ref[...] = x

    h = h_ref[...]

    def branch(a_ref, wg_ref, bg_ref, wu_ref):
        gate = jax.nn.sigmoid(jnp.dot(h, wg_ref[...], preferred_element_type=F32) + bg_ref[...])
        return gate * jnp.dot(a_ref[...], wu_ref[...], preferred_element_type=F32)

    y = (branch(pool_ref, wg0_ref, bg0_ref, wup_ref)
         + branch(diff_ref, wg1_ref, bg1_ref, wud_ref)
         + branch(mem_ref, wg2_ref, bg2_ref, wum_ref))
    acc_ref[...] += jnp.dot(y.astype(BF16), wo_ref[...], preferred_element_type=F32)

    @pl.when(n == pl.num_programs(1) - 1)
    def _():
        o_ref[...] = acc_ref[...]


def _merge(x2, norm_w, pool_out, diff_out, mem_out, w_gate, b_gate, w_up_pool, w_up_diff, w_up_mem, w_out,
           *, tm, tn):
    t, d = x2.shape
    nn = d // tn
    rows = lambda width: pl.BlockSpec((tm, width), lambda i, n: (i, 0))
    gate_w = lambda br: pl.BlockSpec((d, tn), lambda i, n: (0, br * nn + n))
    gate_b = lambda br: pl.BlockSpec((1, tn), lambda i, n: (0, br * nn + n))
    up_w = lambda width: pl.BlockSpec((width, tn), lambda i, n: (0, n))
    return pl.pallas_call(
        _merge_kernel,
        out_shape=jax.ShapeDtypeStruct((t, d), F32),
        grid=(t // tm, nn),
        in_specs=[
            rows(d),
            pl.BlockSpec((1, d), lambda i, n: (0, 0)),
            rows(pool_out.shape[1]), rows(diff_out.shape[1]), rows(mem_out.shape[1]),
            gate_w(0), gate_w(1), gate_w(2),
            gate_b(0), gate_b(1), gate_b(2),
            up_w(pool_out.shape[1]), up_w(diff_out.shape[1]), up_w(mem_out.shape[1]),
            pl.BlockSpec((tn, d), lambda i, n: (n, 0)),
        ],
        out_specs=pl.BlockSpec((tm, d), lambda i, n: (i, 0)),
        scratch_shapes=[pltpu.VMEM((tm, d), BF16), pltpu.VMEM((tm, d), F32)],
        compiler_params=_cparams("parallel", "arbitrary"),
        name="merge",
    )(x2, norm_w, pool_out, diff_out, mem_out, w_gate, w_gate, w_gate, b_gate, b_gate, b_gate,
      w_up_pool, w_up_diff, w_up_mem, w_out)


def _split_bf16(a):
    hi = a.astype(BF16)
    lo = (a - hi.astype(F32)).astype(BF16)
    return hi, lo


def _router_kernel(x_ref, nw_ref, whi_ref, wlo_ref, b_ref, idx_ref, gate_ref):
    h = _rms_rows(x_ref[...], nw_ref[...])
    hi, lo = _split_bf16(h)
    logits = (jnp.dot(hi, whi_ref[...], preferred_element_type=F32)
              + jnp.dot(hi, wlo_ref[...], preferred_element_type=F32)
              + jnp.dot(lo, whi_ref[...], preferred_element_type=F32)) + b_ref[...]
    tm, ne = logits.shape
    eid = lax.broadcasted_iota(jnp.int32, (tm, ne), 1)
    vals, ids = [], []
    cur = logits
    for _ in range(TOP_K):
        mx = jnp.max(cur, axis=-1, keepdims=True)
        sel = jnp.min(jnp.where(cur == mx, eid, ne), axis=-1, keepdims=True)
        vals.append(mx)
        ids.append(sel)
        cur = jnp.where(eid == sel, -jnp.inf, cur)
    exps = [jnp.exp(v - vals[0]) for v in vals]
    denom = exps[0] + exps[1] + exps[2] + exps[3]
    lane = lax.broadcasted_iota(jnp.int32, (tm, LANES), 1)
    idx_out = jnp.zeros((tm, LANES), jnp.int32)
    gate_out = jnp.zeros((tm, LANES), F32)
    for k in range(TOP_K):
        idx_out = jnp.where(lane == k, ids[k], idx_out)
        gate_out = jnp.where(lane == k, exps[k] / denom, gate_out)
    idx_ref[...] = idx_out
    gate_ref[...] = gate_out


def _router(x1, norm_w, w_hi, w_lo, bias, *, tm):
    t, d = x1.shape
    ne = w_hi.shape[1]
    return pl.pallas_call(
        _router_kernel,
        out_shape=(jax.ShapeDtypeStruct((t, LANES), jnp.int32), jax.ShapeDtypeStruct((t, LANES), F32)),
        grid=(t // tm,),
        in_specs=[
            pl.BlockSpec((tm, d), lambda i: (i, 0)),
            pl.BlockSpec((1, d), lambda i: (0, 0)),
            pl.BlockSpec((d, ne), lambda i: (0, 0)),
            pl.BlockSpec((d, ne), lambda i: (0, 0)),
            pl.BlockSpec((1, ne), lambda i: (0, 0)),
        ],
        out_specs=(pl.BlockSpec((tm, LANES), lambda i: (i, 0)),
                   pl.BlockSpec((tm, LANES), lambda i: (i, 0))),
        compiler_params=_cparams("parallel"),
        name="router",
    )(x1, norm_w, w_hi, w_lo, bias)


def _moe_kernel(be_ref, na_ref, tok_ref, tokn_ref, dstp_ref, x_hbm, nw_ref,
                wg_ref, wu_ref, bg_ref, bu_ref, wd_ref, bd_ref, y_hbm,
                xbuf0, xbuf1, xn_ref, acc_ref, ybuf0, ybuf1, gsem, ssem, *, tm, nf):
    b = pl.program_id(0)
    f = pl.program_id(1)
    nb = pl.num_programs(0)
    nact = na_ref[0]
    slot = lax.rem(b, 2)
    xbufs = (xbuf0, xbuf1)
    ybufs = (ybuf0, ybuf1)

    def gather_row(idx_ref, s, r, prio=0):
        t = idx_ref[0, 0, r]
        pltpu.make_async_copy(x_hbm.at[pl.ds(t, 1)], xbufs[s].at[pl.ds(r, 1)], gsem.at[s]).start(priority=prio)

    def scatter_row(s, r, prio=0):
        t = dstp_ref[0, 0, r]
        pltpu.make_async_copy(ybufs[s].at[pl.ds(r, 1)], y_hbm.at[pl.ds(t, 1)], ssem.at[s]).start(priority=prio)

    def gather_wait(s):
        pltpu.make_async_copy(x_hbm.at[pl.ds(0, tm)], xbufs[s], gsem.at[s]).wait()

    def scatter_wait(s):
        pltpu.make_async_copy(ybufs[s], y_hbm.at[pl.ds(0, tm)], ssem.at[s]).wait()

    def all_rows(fn):
        def one(r, c):
            fn(r)
            return c
        lax.fori_loop(0, tm, one, 0, unroll=8)

    active = b < nact

    @pl.when((b == 0) & (f == 0))
    def _():
        all_rows(lambda r: gather_row(tok_ref, 0, r))

    for s in (0, 1):
        @pl.when((b == nact) & (f == 0) & (slot == s))
        def _():
            gather_wait(s)
            all_rows(lambda r: scatter_row(1 - s, r))

        @pl.when(active & (f == 0) & (slot == s))
        def _():
            gather_wait(s)
            for r in range(tm):
                gather_row(tokn_ref, 1 - s, r, r % 2)
            xn_ref[...] = _rms_rows(xbufs[s][...], nw_ref[...]).astype(BF16)

    @pl.when(active)
    def _():
        xn = xn_ref[...]
        gate = jnp.dot(xn, wg_ref[0].astype(BF16), preferred_element_type=F32) + bg_ref[0]
        up = jnp.dot(xn, wu_ref[0].astype(BF16), preferred_element_type=F32) + bu_ref[0]
        gate = jnp.minimum(gate, SWIGLU_LIMIT)
        up = jnp.clip(up, -SWIGLU_LIMIT, SWIGLU_LIMIT)
        act = gate * jax.nn.sigmoid(gate * SWIGLU_ALPHA) * (up + 1.0)
        part = jnp.dot(act.astype(BF16), wd_ref[0].astype(BF16), preferred_element_type=F32)
        acc_ref[...] = jnp.where(f == 0, part, acc_ref[...] + part)

    for s in (0, 1):
        last = active & (f == nf - 1) & (slot == s)

        @pl.when(last & (b >= 2))
        def _():
            scatter_wait(s)

        @pl.when(last & (b >= 1))
        def _():
            for r in range(tm):
                scatter_row(1 - s, r, r % 2)
            ybufs[s][...] = acc_ref[...] + bd_ref[0]

        @pl.when(last & (b == 0))
        def _():
            ybufs[s][...] = acc_ref[...] + bd_ref[0]

    @pl.when((b == nb - 1) & (f == nf - 1))
    def _():
        for s in (0, 1):
            @pl.when((nact >= 2) | (lax.rem(nact - 1, 2) == s))
            def _():
                scatter_wait(s)


def _moe(blk_expert, nact, src_tok, dst_row, x1, norm_w, w_gu, b_gu, w_down, b_down, *, tm, tf, n_out_rows):
    t, d = x1.shape
    ne, _, two_f = w_gu.shape
    nf = two_f // 2 // tf
    nblk = src_tok.shape[0]

    def eidx(b, be, na):
        return be[jnp.minimum(b, na[0] - 1)]

    def fidx(b, f, na):
        return jnp.where(b < na[0], f, nf - 1)

    idx_blk = lambda shift: pl.BlockSpec(
        (1, 1, tm), lambda b, f, be, na: (jnp.clip(b + shift, 0, nblk - 1), 0, 0), memory_space=pltpu.SMEM)
    grid_spec = pltpu.PrefetchScalarGridSpec(
        num_scalar_prefetch=2,
        grid=(nblk, nf),
        in_specs=[
            idx_blk(0), idx_blk(1), idx_blk(-1),
            pl.BlockSpec(memory_space=pl.ANY),
            pl.BlockSpec((1, d), lambda b, f, be, na: (0, 0)),
            pl.BlockSpec((1, d, tf), lambda b, f, be, na: (eidx(b, be, na), 0, fidx(b, f, na))),
            pl.BlockSpec((1, d, tf), lambda b, f, be, na: (eidx(b, be, na), 0, nf + fidx(b, f, na))),
            pl.BlockSpec((1, 1, tf), lambda b, f, be, na: (eidx(b, be, na), 0, fidx(b, f, na))),
            pl.BlockSpec((1, 1, tf), lambda b, f, be, na: (eidx(b, be, na), 0, nf + fidx(b, f, na))),
            pl.BlockSpec((1, tf, d), lambda b, f, be, na: (eidx(b, be, na), fidx(b, f, na), 0)),
            pl.BlockSpec((1, 1, d), lambda b, f, be, na: (eidx(b, be, na), 0, 0)),
        ],
        out_specs=pl.BlockSpec(memory_space=pl.ANY),
        scratch_shapes=[
            pltpu.VMEM((tm, d), F32),
            pltpu.VMEM((tm, d), F32),
            pltpu.VMEM((tm, d), BF16),
            pltpu.VMEM((tm, d), F32),
            pltpu.VMEM((tm, d), F32),
            pltpu.VMEM((tm, d), F32),
            pltpu.SemaphoreType.DMA((2,)),
            pltpu.SemaphoreType.DMA((2,)),
        ],
    )
    return pl.pallas_call(
        functools.partial(_moe_kernel, tm=tm, nf=nf),
        out_shape=jax.ShapeDtypeStruct((n_out_rows, d), F32),
        grid_spec=grid_spec,
        compiler_params=_cparams("arbitrary", "arbitrary"),
        name="moe",
    )(blk_expert, nact, src_tok, src_tok, dst_row, x1, norm_w, w_gu, w_gu, b_gu, b_gu, w_down, b_down)


def _combine_kernel(x_ref, g_ref, y0_ref, y1_ref, y2_ref, y3_ref, o_ref):
    g = g_ref[...]
    out = x_ref[...]
    for k, y_ref in enumerate((y0_ref, y1_ref, y2_ref, y3_ref)):
        out = out + g[:, k:k + 1] * y_ref[...]
    o_ref[...] = out


def _combine(x1, gates, yk, *, tc):
    t, d = x1.shape
    nt = t // tc
    ysp = lambda k: pl.BlockSpec((tc, d), lambda i: (k * nt + i, 0))
    return pl.pallas_call(
        _combine_kernel,
        out_shape=jax.ShapeDtypeStruct((t, d), F32),
        grid=(nt,),
        in_specs=[
            pl.BlockSpec((tc, d), lambda i: (i, 0)),
            pl.BlockSpec((tc, LANES), lambda i: (i, 0)),
            ysp(0), ysp(1), ysp(2), ysp(3),
        ],
        out_specs=pl.BlockSpec((tc, d), lambda i: (i, 0)),
        compiler_params=_cparams("parallel"),
        name="combine",
    )(x1, gates, yk, yk, yk, yk)


def _routing_plan(top_idx, n_experts, tm):
    t = top_idx.shape[0]
    tk = t * TOP_K
    nblk = tk // tm + n_experts
    flat_e = top_idx.reshape(tk)
    order = jnp.argsort(flat_e).astype(jnp.int32)
    experts = jnp.arange(n_experts, dtype=jnp.int32)
    counts = jnp.sum((flat_e[:, None] == experts[None, :]).astype(jnp.int32), axis=0)
    padded = ((counts + tm - 1) // tm) * tm
    start = jnp.cumsum(counts) - counts
    pcum = jnp.cumsum(padded)
    pstart = pcum - padded
    blk_start = jnp.arange(nblk, dtype=jnp.int32) * tm
    blk_expert = jnp.minimum(jnp.sum((pcum[None, :] <= blk_start[:, None]).astype(jnp.int32), axis=1),
                             n_experts - 1)
    rank = (blk_start - pstart[blk_expert])[:, None] + jnp.arange(tm, dtype=jnp.int32)[None, :]
    cnt = counts[blk_expert][:, None]
    valid = rank < cnt
    row_flat = order[jnp.minimum(start[blk_expert][:, None] + rank, tk - 1)]
    tok = row_flat // TOP_K
    kk = row_flat - tok * TOP_K
    src_tok = jnp.where(valid, tok, 0)
    dst_row = jnp.where(valid, kk * t + tok, tk + blk_expert[:, None] * tm + (rank - cnt))
    nact = (pcum[-1] // tm).astype(jnp.int32).reshape(1)
    return (blk_expert, nact, src_tok.reshape(nblk, 1, tm), dst_row.reshape(nblk, 1, tm), tk + n_experts * tm)


def kernel(x, mem, norm_mix_w, norm_mem_w, w_in, w_gate, b_gate, pool_w, pool_scale, q_norm_w, k_norm_w,
           lambda_q1, lambda_k1, lambda_q2, lambda_k2, diff_norm_w, rel_bias, w_mem_kv, mem_q_norm_w,
           mem_k_norm_w, w_up_pool, w_up_diff, w_up_mem, w_out, norm_ffn_w, router_w, router_b, w_gu, b_gu,
           w_down, b_down):
    batch, seq, d = x.shape
    mlen = mem.shape[1]
    t = batch * seq
    assert norm_mix_w.shape[0] == 1, "single-layer block"
    n_experts = router_w.shape[2]

    pool_width = len(POOL_WINDOWS) * POOL_GROUP_DIM
    qk_width = DIFF_HEADS * 2 * DIFF_QK_DIM
    v_width = DIFF_HEADS * DIFF_V_DIM
    mem_width = MEM_HEADS * MEM_HEAD_DIM
    q_col = pool_width
    k_col = q_col + qk_width
    v_col = k_col + qk_width
    mq_col = v_col + v_width
    chunk = 512
    assert w_in.shape[2] == mq_col + mem_width
    segs = (("p",) * (pool_width // chunk) + ("q",) * (qk_width // chunk) + ("k",) * (qk_width // chunk)
            + ("v",) * (v_width // chunk) + ("m",) * (mem_width // chunk))

    x2 = x.reshape(t, d)
    row = lambda a: a.reshape(1, -1)

    half = chunk // 2
    gmat = jnp.asarray(np.kron(np.eye(half // DIFF_QK_DIM), np.ones((DIFF_QK_DIM, DIFF_QK_DIM))), BF16)
    reps = chunk // DIFF_QK_DIM
    qw = row(jnp.tile(q_norm_w[0] * (1.0 / math.sqrt(DIFF_QK_DIM)), reps))
    kw = row(jnp.tile(k_norm_w[0], reps))
    mw = row(mem_q_norm_w[0] * (1.0 / math.sqrt(MEM_HEAD_DIM)))
    proj = _proj(x2, row(norm_mix_w[0]), w_in[0].astype(BF16), gmat, qw, kw, mw, tm=512, segs=segs, chunk=chunk)

    pool_out = _pool(proj, pool_w[0].astype(BF16), row(pool_scale[0]), batch=batch, seq=seq)

    rel_flat = rel_bias.reshape(-1)
    tables = _bias_tables(rel_flat, jnp.asarray(_bucket_tables(ATT_BLOCK)))
    lam_vecs = jnp.concatenate([lambda_q1, lambda_k1, lambda_q2, lambda_k2], axis=0)
    diff_out = _diff_attention(proj, tables, lam_vecs, row(diff_norm_w[0]),
                               batch=batch, seq=seq, q_col=q_col, k_col=k_col, v_col=v_col, hp=4)

    mk, mv = _mem_kv(mem.reshape(batch * mlen, d), row(norm_mem_w[0]), w_mem_kv[0].astype(BF16),
                     row(mem_k_norm_w[0]), batch=batch, mlen=mlen)
    mem_out = _mem_attention(proj, mk, mv, batch=batch, seq=seq, mlen=mlen, q_col=mq_col, tm=512)

    x1 = _merge(x2, row(norm_mix_w[0]), pool_out, diff_out, mem_out, w_gate[0].astype(BF16), row(b_gate[0]),
                w_up_pool[0].astype(BF16), w_up_diff[0].astype(BF16), w_up_mem[0].astype(BF16),
                w_out[0].astype(BF16), tm=512, tn=512)

    rw = router_w[0]
    rw_hi = rw.astype(BF16)
    rw_lo = (rw - rw_hi.astype(F32)).astype(BF16)
    idx_pad, gate_pad = _router(x1, row(norm_ffn_w[0]), rw_hi, rw_lo, row(router_b[0]), tm=512)
    tm_moe = 512
    blk_expert, nact, src_tok, dst_row, n_out_rows = _routing_plan(idx_pad[:, :TOP_K], n_experts, tm_moe)
    yk = _moe(blk_expert, nact, src_tok, dst_row, x1, row(norm_ffn_w[0]),
              w_gu[0], b_gu[0].reshape(n_experts, 1, -1),
              w_down[0], b_down[0].reshape(n_experts, 1, -1),
              tm=tm_moe, tf=512, n_out_rows=n_out_rows)
    out = _combine(x1, gate_pad, yk, tc=256)
    return out.reshape(batch, seq, d)
```

```python
import functools
import math

import numpy as np
import jax
import jax.numpy as jnp
from jax import lax
from jax.experimental import pallas as pl
from jax.experimental.pallas import tpu as pltpu

F32 = jnp.float32
BF16 = jnp.bfloat16

EPS = 1e-6
DIFF_HEADS = 8
DIFF_QK_DIM = 64
DIFF_V_DIM = 128
MEM_HEADS = 4
MEM_HEAD_DIM = 128
POOL_WINDOWS = (2, 4, 8, 16)
POOL_GROUP_DIM = 128
NUM_BUCKETS = 32
MAX_DISTANCE = 128
TOP_K = 4
SWIGLU_LIMIT = 7.0
SWIGLU_ALPHA = 1.702
LAM_INIT = 0.8 - 0.6 * math.exp(-0.3 * 0)

LANES = 128
VMEM_LIMIT = 56 * 1024 * 1024

ATT_BLOCK = 512
NT_DIMS = (((1,), (1,)), ((), ()))


def _cparams(*sem):
    return pltpu.CompilerParams(dimension_semantics=sem, vmem_limit_bytes=VMEM_LIMIT)


def _rms_rows(x, w):
    ms = jnp.mean(x * x, axis=-1, keepdims=True)
    return x * lax.rsqrt(ms + EPS) * w


def _proj_kernel(x_ref, nw_ref, w_ref, g_ref, qw_ref, kw_ref, mw_ref, o_ref, h_ref, *, chunk, segs):
    h_ref[...] = _rms_rows(x_ref[...], nw_ref[...]).astype(BF16)
    n = w_ref.shape[1]
    half = chunk // 2
    for c in range(n // chunk):
        c0 = c * chunk
        y = jnp.dot(h_ref[...], w_ref[:, c0:c0 + chunk], preferred_element_type=F32)
        kind = segs[c]
        if kind in ("q", "k"):
            wn = (qw_ref if kind == "q" else kw_ref)[...]
            outs = []
            for s in range(2):
                ys = y[:, s * half:(s + 1) * half]
                ss = jnp.dot((ys * ys).astype(BF16), g_ref[...], preferred_element_type=F32)
                outs.append(ys * lax.rsqrt(ss * (1.0 / DIFF_QK_DIM) + EPS) * wn[:, s * half:(s + 1) * half])
            y = jnp.concatenate(outs, axis=-1)
        elif kind == "m":
            outs = []
            for s in range(chunk // MEM_HEAD_DIM):
                ys = y[:, s * MEM_HEAD_DIM:(s + 1) * MEM_HEAD_DIM]
                outs.append(_rms_rows(ys, mw_ref[...]))
            y = jnp.concatenate(outs, axis=-1)
        o_ref[:, c0:c0 + chunk] = y.astype(BF16)


def _proj(x2, norm_w, w_in, gmat, qw, kw, mw, *, tm, segs, chunk):
    t, d = x2.shape
    n = w_in.shape[1]
    full = lambda shape: pl.BlockSpec(shape, lambda i: (0,) * len(shape))
    return pl.pallas_call(
        functools.partial(_proj_kernel, chunk=chunk, segs=segs),
        out_shape=jax.ShapeDtypeStruct((t, n), BF16),
        grid=(t // tm,),
        in_specs=[
            pl.BlockSpec((tm, d), lambda i: (i, 0)),
            full((1, d)),
            pl.BlockSpec((d, n), lambda i: (0, 0), pipeline_mode=pl.Buffered(1)),
            full(gmat.shape),
            full(qw.shape),
            full(kw.shape),
            full(mw.shape),
        ],
        out_specs=pl.BlockSpec((tm, n), lambda i: (i, 0)),
        scratch_shapes=[pltpu.VMEM((tm, d), BF16)],
        compiler_params=_cparams("parallel"),
        name="proj",
    )(x2, norm_w, w_in, gmat, qw, kw, mw)


def _pool_kernel(u_ref, pw_ref, ps_ref, o_ref):
    s_len = u_ref.shape[0]
    row = lax.broadcasted_iota(jnp.int32, (s_len, POOL_GROUP_DIM), 0)
    for g, w in enumerate(POOL_WINDOWS):
        cols = slice(g * POOL_GROUP_DIM, (g + 1) * POOL_GROUP_DIM)
        ug = u_ref[:, cols].astype(F32)
        acc = ug
        span = 1
        while span < w:
            shifted = pltpu.roll(acc, span, 0)
            acc = acc + jnp.where(row >= span, shifted, 0.0)
            span *= 2
        cnt = jnp.minimum(row + 1, w).astype(F32)
        pooled = acc / cnt - ug
        mixed = jnp.dot(pooled.astype(BF16), pw_ref[g], preferred_element_type=F32)
        o_ref[:, cols] = (mixed * ps_ref[:, cols]).astype(BF16)


def _pool(proj, pool_w, pool_scale, *, batch, seq):
    width = len(POOL_WINDOWS) * POOL_GROUP_DIM
    return pl.pallas_call(
        _pool_kernel,
        out_shape=jax.ShapeDtypeStruct((batch * seq, width), BF16),
        grid=(batch,),
        in_specs=[
            pl.BlockSpec((seq, width), lambda b: (b, 0)),
            pl.BlockSpec(pool_w.shape, lambda b: (0, 0, 0)),
            pl.BlockSpec((1, width), lambda b: (0, 0)),
        ],
        out_specs=pl.BlockSpec((seq, width), lambda b: (b, 0)),
        compiler_params=_cparams("parallel"),
        name="pool",
    )(proj, pool_w, pool_scale)


def _bucket_tables(blk):
    exact = NUM_BUCKETS // 2
    kj = np.arange(blk)[:, None]
    qi = np.arange(blk)[None, :]
    out = []
    for off in (0, 1):
        rel = qi - kj + off * blk
        n = np.maximum(rel, 0)
        nf = np.maximum(n, 1).astype(np.float64)
        large = exact + (np.log(nf / exact) / math.log(MAX_DISTANCE / exact) * (NUM_BUCKETS - exact)).astype(np.int64)
        large = np.minimum(large, NUM_BUCKETS - 1)
        bucket = np.where(n < exact, n, large)
        out.append(np.where(rel >= 0, bucket, -1))
    return np.stack(out).astype(np.int32)


def _bias_kernel(rb_ref, bucket_ref, o_ref):
    h = pl.program_id(0)
    far = rb_ref[(NUM_BUCKETS - 1) * DIFF_HEADS + h]
    for t in range(2):
        bk = bucket_ref[t]
        acc = jnp.full(bk.shape, -jnp.inf, F32)
        for b in range(NUM_BUCKETS):
            acc = jnp.where(bk == b, rb_ref[b * DIFF_HEADS + h] - far, acc)
        o_ref[0, t] = acc


def _bias_tables(rel_bias_flat, buckets):
    _, blk, _ = buckets.shape
    return pl.pallas_call(
        _bias_kernel,
        out_shape=jax.ShapeDtypeStruct((DIFF_HEADS, 2, blk, blk), F32),
        grid=(DIFF_HEADS,),
        in_specs=[
            pl.BlockSpec(memory_space=pltpu.SMEM),
            pl.BlockSpec(buckets.shape, lambda h: (0, 0, 0)),
        ],
        out_specs=pl.BlockSpec((1, 2, blk, blk), lambda h: (h, 0, 0, 0)),
        compiler_params=_cparams("arbitrary"),
        name="bias_tables",
    )(rel_bias_flat, buckets)


def _attn_kernel(q_ref, k_ref, v_ref, tbl_ref, lam_ref, nw_ref, o_ref, vt_ref, *, blk, hp):
    i = pl.program_id(2)
    nk = vt_ref.shape[1]
    heads = range(hp)
    hcols = lambda h: slice(h * LANES, (h + 1) * LANES)

    @pl.when(i == 0)
    def _():
        for h in heads:
            for c in range(nk):
                vt_ref[h, c] = v_ref[c * blk:(c + 1) * blk, hcols(h)].astype(F32).T.astype(BF16)

    lane = lax.broadcasted_iota(jnp.int32, (blk, LANES), 1)
    zero = jnp.zeros((blk, LANES), BF16)
    qq = []
    for h in heads:
        q = q_ref[:, hcols(h)]
        qq.append(jnp.concatenate([jnp.where(lane < DIFF_QK_DIM, q, zero),
                                   jnp.where(lane >= DIFF_QK_DIM, q, zero)], axis=0))

    def step(j, carry, biased):
        off = pl.multiple_of(j * blk, blk)
        scores = [lax.dot_general(k_ref[pl.ds(off, blk), hcols(h)], qq[h], NT_DIMS, preferred_element_type=F32)
                  for h in heads]
        stats, probs = [], []
        for h in heads:
            m, l, _ = carry[h]
            s = scores[h]
            if biased:
                bias = tbl_ref[h, i - j]
                s = s + jnp.concatenate([bias, bias], axis=1)
            m_new = jnp.maximum(m, jnp.max(s, axis=0, keepdims=True))
            p = jnp.exp(s - m_new)
            alpha = jnp.exp(m - m_new)
            stats.append((m_new, alpha * l + jnp.sum(p, axis=0, keepdims=True), alpha))
            probs.append(p.astype(BF16))
        pvs = [jnp.dot(vt_ref[h, j], probs[h], preferred_element_type=F32) for h in heads]
        return tuple((stats[h][0], stats[h][1], stats[h][2] * carry[h][2] + pvs[h]) for h in heads)

    init = tuple((jnp.full((1, 2 * blk), -jnp.inf, F32), jnp.zeros((1, 2 * blk), F32),
                  jnp.zeros((DIFF_V_DIM, 2 * blk), F32)) for _ in heads)
    carry = lax.fori_loop(0, i - 1, lambda j, c: step(j, c, False), init)
    carry = lax.fori_loop(jnp.maximum(i - 1, 0), i + 1, lambda j, c: step(j, c, True), carry)

    lv = lam_ref[...]
    lam = (jnp.exp(jnp.sum(lv[0:1] * lv[1:2], axis=-1, keepdims=True))
           - jnp.exp(jnp.sum(lv[2:3] * lv[3:4], axis=-1, keepdims=True)) + LAM_INIT)
    for h in heads:
        _, l, acc = carry[h]
        on = acc / l
        o = (on[:, :blk] - lam * on[:, blk:]).T
        o_ref[:, hcols(h)] = (_rms_rows(o, nw_ref[...]) * (1.0 - LAM_INIT)).astype(BF16)


def _diff_attention(proj, tables, lam_vecs, diff_norm_w, *, batch, seq, q_col, k_col, v_col, hp):
    blk = ATT_BLOCK
    nq = seq // blk
    width = hp * LANES
    qcb, kcb, vcb = q_col // width, k_col // width, v_col // width
    return pl.pallas_call(
        functools.partial(_attn_kernel, blk=blk, hp=hp),
        out_shape=jax.ShapeDtypeStruct((batch * seq, DIFF_HEADS * DIFF_V_DIM), BF16),
        grid=(batch, DIFF_HEADS // hp, nq),
        in_specs=[
            pl.BlockSpec((blk, width), lambda b, g, i: (b * nq + i, qcb + g)),
            pl.BlockSpec((seq, width), lambda b, g, i: (b, kcb + g)),
            pl.BlockSpec((seq, width), lambda b, g, i: (b, vcb + g)),
            pl.BlockSpec((hp, 2, blk, blk), lambda b, g, i: (g, 0, 0, 0)),
            pl.BlockSpec(lam_vecs.shape, lambda b, g, i: (0, 0)),
            pl.BlockSpec((1, DIFF_V_DIM), lambda b, g, i: (0, 0)),
        ],
        out_specs=pl.BlockSpec((blk, width), lambda b, g, i: (b * nq + i, g)),
        scratch_shapes=[pltpu.VMEM((hp, nq, DIFF_V_DIM, blk), BF16)],
        compiler_params=_cparams("arbitrary", "arbitrary", "arbitrary"),
        name="diff_attn",
    )(proj, proj, proj, tables, lam_vecs, diff_norm_w)


def _memkv_kernel(m_ref, nw_ref, w_ref, kw_ref, k_ref, v_ref):
    hm = _rms_rows(m_ref[...], nw_ref[...]).astype(BF16)
    kv = jnp.dot(hm, w_ref[...], preferred_element_type=F32)
    width = MEM_HEADS * MEM_HEAD_DIM
    for g in range(MEM_HEADS):
        cols = slice(g * MEM_HEAD_DIM, (g + 1) * MEM_HEAD_DIM)
        k_ref[:, cols] = _rms_rows(kv[:, cols], kw_ref[...]).astype(BF16)
    v_ref[...] = kv[:, width:].astype(BF16)


def _mem_kv(mem2, norm_w, w_kv, k_norm_w, *, batch, mlen):
    d = mem2.shape[1]
    width = MEM_HEADS * MEM_HEAD_DIM
    out = jax.ShapeDtypeStruct((batch * mlen, width), BF16)
    return pl.pallas_call(
        _memkv_kernel,
        out_shape=(out, out),
        grid=(batch,),
        in_specs=[
            pl.BlockSpec((mlen, d), lambda b: (b, 0)),
            pl.BlockSpec((1, d), lambda b: (0, 0)),
            pl.BlockSpec(w_kv.shape, lambda b: (0, 0)),
            pl.BlockSpec((1, MEM_HEAD_DIM), lambda b: (0, 0)),
        ],
        out_specs=(pl.BlockSpec((mlen, width), lambda b: (b, 0)),
                   pl.BlockSpec((mlen, width), lambda b: (b, 0))),
        compiler_params=_cparams("parallel"),
        name="mem_kv",
    )(mem2, norm_w, w_kv, k_norm_w)


def _memattn_kernel(q_ref, k_ref, v_ref, o_ref):
    for g in range(MEM_HEADS):
        cols = slice(g * MEM_HEAD_DIM, (g + 1) * MEM_HEAD_DIM)
        s = lax.dot_general(q_ref[:, cols], k_ref[:, cols], NT_DIMS, preferred_element_type=F32)
        s = s - jnp.max(s, axis=-1, keepdims=True)
        p = jnp.exp(s)
        p = p / jnp.sum(p, axis=-1, keepdims=True)
        o_ref[:, cols] = jnp.dot(p.astype(BF16), v_ref[:, cols], preferred_element_type=F32).astype(BF16)


def _mem_attention(proj, mk, mv, *, batch, seq, mlen, q_col, tm):
    width = MEM_HEADS * MEM_HEAD_DIM
    nt = seq // tm
    qcb = q_col // width
    return pl.pallas_call(
        _memattn_kernel,
        out_shape=jax.ShapeDtypeStruct((batch * seq, width), BF16),
        grid=(batch, nt),
        in_specs=[
            pl.BlockSpec((tm, width), lambda b, i: (b * nt + i, qcb)),
            pl.BlockSpec((mlen, width), lambda b, i: (b, 0)),
            pl.BlockSpec((mlen, width), lambda b, i: (b, 0)),
        ],
        out_specs=pl.BlockSpec((tm, width), lambda b, i: (b * nt + i, 0)),
        compiler_params=_cparams("parallel", "parallel"),
        name="mem_attn",
    )(proj, mk, mv)


def _merge_kernel(x_ref, nw_ref, pool_ref, diff_ref, mem_ref,
                  wg0_ref, wg1_ref, wg2_ref, bg0_ref, bg1_ref, bg2_ref,
                  wup_ref, wud_ref, wum_ref, wo_ref, o_ref, h_ref, acc_ref):
    n = pl.program_id(1)

    @pl.when(n == 0)
    def _():
        x = x_ref[...]
        h_ref[...] = _rms_rows(x, nw_ref[...]).astype(BF16)
        acc_ref[...] = x

    h = h_ref[...]

    def branch(a_ref, wg_ref, bg_ref, wu_ref):
        gate = jax.nn.sigmoid(jnp.dot(h, wg_ref[...], preferred_element_type=F32) + bg_ref[...])
        return gate * jnp.dot(a_ref[...], wu_ref[...], preferred_element_type=F32)

    y = (branch(pool_ref, wg0_ref, bg0_ref, wup_ref)
         + branch(diff_ref, wg1_ref, bg1_ref, wud_ref)
         + branch(mem_ref, wg2_ref, bg2_ref, wum_ref))
    acc_ref[...] += jnp.dot(y.astype(BF16), wo_ref[...], preferred_element_type=F32)

    @pl.when(n == pl.num_programs(1) - 1)
    def _():
        o_ref[...] = acc_ref[...]


def _merge(x2, norm_w, pool_out, diff_out, mem_out, w_gate, b_gate, w_up_pool, w_up_diff, w_up_mem, w_out,
           *, tm, tn):
    t, d = x2.shape
    nn = d // tn
    rows = lambda width: pl.BlockSpec((tm, width), lambda i, n: (i, 0))
    gate_w = lambda br: pl.BlockSpec((d, tn), lambda i, n: (0, br * nn + n))
    gate_b = lambda br: pl.BlockSpec((1, tn), lambda i, n: (0, br * nn + n))
    up_w = lambda width: pl.BlockSpec((width, tn), lambda i, n: (0, n))
    return pl.pallas_call(
        _merge_kernel,
        out_shape=jax.ShapeDtypeStruct((t, d), F32),
        grid=(t // tm, nn),
        in_specs=[
            rows(d),
            pl.BlockSpec((1, d), lambda i, n: (0, 0)),
            rows(pool_out.shape[1]), rows(diff_out.shape[1]), rows(mem_out.shape[1]),
            gate_w(0), gate_w(1), gate_w(2),
            gate_b(0), gate_b(1), gate_b(2),
            up_w(pool_out.shape[1]), up_w(diff_out.shape[1]), up_w(mem_out.shape[1]),
            pl.BlockSpec((tn, d), lambda i, n: (n, 0)),
        ],
        out_specs=pl.BlockSpec((tm, d), lambda i, n: (i, 0)),
        scratch_shapes=[pltpu.VMEM((tm, d), BF16), pltpu.VMEM((tm, d), F32)],
        compiler_params=_cparams("parallel", "arbitrary"),
        name="merge",
    )(x2, norm_w, pool_out, diff_out, mem_out, w_gate, w_gate, w_gate, b_gate, b_gate, b_gate,
      w_up_pool, w_up_diff, w_up_mem, w_out)


def _split_bf16(a):
    hi = a.astype(BF16)
    lo = (a - hi.astype(F32)).astype(BF16)
    return hi, lo


def _router_kernel(x_ref, nw_ref, whi_ref, wlo_ref, b_ref, idx_ref, gate_ref):
    h = _rms_rows(x_ref[...], nw_ref[...])
    hi, lo = _split_bf16(h)
    logits = (jnp.dot(hi, whi_ref[...], preferred_element_type=F32)
              + jnp.dot(hi, wlo_ref[...], preferred_element_type=F32)
              + jnp.dot(lo, whi_ref[...], preferred_element_type=F32)) + b_ref[...]
    tm, ne = logits.shape
    eid = lax.broadcasted_iota(jnp.int32, (tm, ne), 1)
    vals, ids = [], []
    cur = logits
    for _ in range(TOP_K):
        mx = jnp.max(cur, axis=-1, keepdims=True)
        sel = jnp.min(jnp.where(cur == mx, eid, ne), axis=-1, keepdims=True)
        vals.append(mx)
        ids.append(sel)
        cur = jnp.where(eid == sel, -jnp.inf, cur)
    exps = [jnp.exp(v - vals[0]) for v in vals]
    denom = exps[0] + exps[1] + exps[2] + exps[3]
    lane = lax.broadcasted_iota(jnp.int32, (tm, LANES), 1)
    idx_out = jnp.zeros((tm, LANES), jnp.int32)
    gate_out = jnp.zeros((tm, LANES), F32)
    for k in range(TOP_K):
        idx_out = jnp.where(lane == k, ids[k], idx_out)
        gate_out = jnp.where(lane == k, exps[k] / denom, gate_out)
    idx_ref[...] = idx_out
    gate_ref[...] = gate_out


def _router(x1, norm_w, w_hi, w_lo, bias, *, tm):
    t, d = x1.shape
    ne = w_hi.shape[1]
    return pl.pallas_call(
        _router_kernel,
        out_shape=(jax.ShapeDtypeStruct((t, LANES), jnp.int32), jax.ShapeDtypeStruct((t, LANES), F32)),
        grid=(t // tm,),
        in_specs=[
            pl.BlockSpec((tm, d), lambda i: (i, 0)),
            pl.BlockSpec((1, d), lambda i: (0, 0)),
            pl.BlockSpec((d, ne), lambda i: (0, 0)),
            pl.BlockSpec((d, ne), lambda i: (0, 0)),
            pl.BlockSpec((1, ne), lambda i: (0, 0)),
        ],
        out_specs=(pl.BlockSpec((tm, LANES), lambda i: (i, 0)),
                   pl.BlockSpec((tm, LANES), lambda i: (i, 0))),
        compiler_params=_cparams("parallel"),
        name="router",
    )(x1, norm_w, w_hi, w_lo, bias)


def _pack_bf16_pairs(y):
    half = y.shape[1] // 2
    lo = pltpu.bitcast(y[:, :half].astype(BF16).astype(F32), jnp.uint32)
    hi = pltpu.bitcast(y[:, half:].astype(BF16).astype(F32), jnp.uint32)
    return (lo >> 16) | (hi & jnp.uint32(0xFFFF0000))


def _unpack_bf16_pairs(u):
    return pltpu.bitcast(u << 16, F32), pltpu.bitcast(u & jnp.uint32(0xFFFF0000), F32)


def _moe_kernel(be_ref, na_ref, tok_ref, tokn_ref, dstp_ref, x_hbm, nw_ref,
                wg_ref, wu_ref, bg_ref, bu_ref, wd_ref, bd_ref, y_hbm,
                xbuf0, xbuf1, xn_ref, acc_ref, ybuf0, ybuf1, gsem, ssem, *, tm, nf):
    b = pl.program_id(0)
    f = pl.program_id(1)
    nb = pl.num_programs(0)
    nact = na_ref[0]
    slot = lax.rem(b, 2)
    xbufs = (xbuf0, xbuf1)
    ybufs = (ybuf0, ybuf1)

    def gather_row(idx_ref, s, r, prio=0):
        t = idx_ref[0, 0, r]
        pltpu.make_async_copy(x_hbm.at[pl.ds(t, 1)], xbufs[s].at[pl.ds(r, 1)], gsem.at[s]).start(priority=prio)

    def scatter_row(s, r, prio=0):
        t = dstp_ref[0, 0, r]
        pltpu.make_async_copy(ybufs[s].at[pl.ds(r, 1)], y_hbm.at[pl.ds(t, 1)], ssem.at[s]).start(priority=prio)

    def gather_wait(s):
        pltpu.make_async_copy(x_hbm.at[pl.ds(0, tm)], xbufs[s], gsem.at[s]).wait()

    def scatter_wait(s):
        pltpu.make_async_copy(ybufs[s], y_hbm.at[pl.ds(0, tm)], ssem.at[s]).wait()

    def all_rows(fn):
        def one(r, c):
            fn(r)
            return c
        lax.fori_loop(0, tm, one, 0, unroll=8)

    active = b < nact

    @pl.when((b == 0) & (f == 0))
    def _():
        all_rows(lambda r: gather_row(tok_ref, 0, r))

    for s in (0, 1):
        @pl.when((b == nact) & (f == 0) & (slot == s))
        def _():
            gather_wait(s)
            all_rows(lambda r: scatter_row(1 - s, r))

        @pl.when(active & (f == 0) & (slot == s))
        def _():
            gather_wait(s)
            for r in range(tm):
                gather_row(tokn_ref, 1 - s, r, r % 2)
            xn_ref[...] = _rms_rows(xbufs[s][...], nw_ref[...]).astype(BF16)

    @pl.when(active)
    def _():
        xn = xn_ref[...]
        gate = jnp.dot(xn, wg_ref[0].astype(BF16), preferred_element_type=F32) + bg_ref[0]
        up = jnp.dot(xn, wu_ref[0].astype(BF16), preferred_element_type=F32) + bu_ref[0]
        gate = jnp.minimum(gate, SWIGLU_LIMIT)
        up = jnp.clip(up, -SWIGLU_LIMIT, SWIGLU_LIMIT)
        act = gate * jax.nn.sigmoid(gate * SWIGLU_ALPHA) * (up + 1.0)
        part = jnp.dot(act.astype(BF16), wd_ref[0].astype(BF16), preferred_element_type=F32)
        acc_ref[...] = jnp.where(f == 0, part, acc_ref[...] + part)

    for s in (0, 1):
        last = active & (f == nf - 1) & (slot == s)

        @pl.when(last & (b >= 2))
        def _():
            scatter_wait(s)

        @pl.when(last & (b >= 1))
        def _():
            for r in range(tm):
                scatter_row(1 - s, r, r % 2)
            ybufs[s][...] = _pack_bf16_pairs(acc_ref[...] + bd_ref[0])

        @pl.when(last & (b == 0))
        def _():
            ybufs[s][...] = _pack_bf16_pairs(acc_ref[...] + bd_ref[0])

    @pl.when((b == nb - 1) & (f == nf - 1))
    def _():
        for s in (0, 1):
            @pl.when((nact >= 2) | (lax.rem(nact - 1, 2) == s))
            def _():
                scatter_wait(s)


def _moe(blk_expert, nact, src_tok, dst_row, x1, norm_w, w_gu, b_gu, w_down, b_down, *, tm, tf, n_out_rows):
    t, d = x1.shape
    ne, _, two_f = w_gu.shape
    nf = two_f // 2 // tf
    nblk = src_tok.shape[0]

    def eidx(b, be, na):
        return be[jnp.minimum(b, na[0] - 1)]

    def fidx(b, f, na):
        return jnp.where(b < na[0], f, nf - 1)

    idx_blk = lambda shift: pl.BlockSpec(
        (1, 1, tm), lambda b, f, be, na: (jnp.clip(b + shift, 0, nblk - 1), 0, 0), memory_space=pltpu.SMEM)
    grid_spec = pltpu.PrefetchScalarGridSpec(
        num_scalar_prefetch=2,
        grid=(nblk, nf),
        in_specs=[
            idx_blk(0), idx_blk(1), idx_blk(-1),
            pl.BlockSpec(memory_space=pl.ANY),
            pl.BlockSpec((1, d), lambda b, f, be, na: (0, 0)),
            pl.BlockSpec((1, d, tf), lambda b, f, be, na: (eidx(b, be, na), 0, fidx(b, f, na))),
            pl.BlockSpec((1, d, tf), lambda b, f, be, na: (eidx(b, be, na), 0, nf + fidx(b, f, na))),
            pl.BlockSpec((1, 1, tf), lambda b, f, be, na: (eidx(b, be, na), 0, fidx(b, f, na))),
            pl.BlockSpec((1, 1, tf), lambda b, f, be, na: (eidx(b, be, na), 0, nf + fidx(b, f, na))),
            pl.BlockSpec((1, tf, d), lambda b, f, be, na: (eidx(b, be, na), fidx(b, f, na), 0)),
            pl.BlockSpec((1, 1, d), lambda b, f, be, na: (eidx(b, be, na), 0, 0)),
        ],
        out_specs=pl.BlockSpec(memory_space=pl.ANY),
        scratch_shapes=[
            pltpu.VMEM((tm, d), F32),
            pltpu.VMEM((tm, d), F32),
            pltpu.VMEM((tm, d), BF16),
            pltpu.VMEM((tm, d), F32),
            pltpu.VMEM((tm, d // 2), jnp.uint32),
            pltpu.VMEM((tm, d // 2), jnp.uint32),
            pltpu.SemaphoreType.DMA((2,)),
            pltpu.SemaphoreType.DMA((2,)),
        ],
    )
    return pl.pallas_call(
        functools.partial(_moe_kernel, tm=tm, nf=nf),
        out_shape=jax.ShapeDtypeStruct((n_out_rows, d // 2), jnp.uint32),
        grid_spec=grid_spec,
        compiler_params=_cparams("arbitrary", "arbitrary"),
        name="moe",
    )(blk_expert, nact, src_tok, src_tok, dst_row, x1, norm_w, w_gu, w_gu, b_gu, b_gu, w_down, b_down)


def _combine_kernel(x_ref, g_ref, y0_ref, y1_ref, y2_ref, y3_ref, o_ref):
    g = g_ref[...]
    half = x_ref.shape[1] // 2
    out_lo = x_ref[:, :half]
    out_hi = x_ref[:, half:]
    for k, y_ref in enumerate((y0_ref, y1_ref, y2_ref, y3_ref)):
        lo, hi = _unpack_bf16_pairs(y_ref[...])
        out_lo = out_lo + g[:, k:k + 1] * lo
        out_hi = out_hi + g[:, k:k + 1] * hi
    o_ref[:, :half] = out_lo
    o_ref[:, half:] = out_hi


def _combine(x1, gates, yk, *, tc):
    t, d = x1.shape
    nt = t // tc
    ysp = lambda k: pl.BlockSpec((tc, d // 2), lambda i: (k * nt + i, 0))
    return pl.pallas_call(
        _combine_kernel,
        out_shape=jax.ShapeDtypeStruct((t, d), F32),
        grid=(nt,),
        in_specs=[
            pl.BlockSpec((tc, d), lambda i: (i, 0)),
            pl.BlockSpec((tc, LANES), lambda i: (i, 0)),
            ysp(0), ysp(1), ysp(2), ysp(3),
        ],
        out_specs=pl.BlockSpec((tc, d), lambda i: (i, 0)),
        compiler_params=_cparams("parallel"),
        name="combine",
    )(x1, gates, yk, yk, yk, yk)


def _routing_plan(top_idx, n_experts, tm):
    t = top_idx.shape[0]
    tk = t * TOP_K
    nblk = tk // tm + n_experts
    flat_e = top_idx.reshape(tk)
    order = jnp.argsort(flat_e).astype(jnp.int32)
    experts = jnp.arange(n_experts, dtype=jnp.int32)
    counts = jnp.sum((flat_e[:, None] == experts[None, :]).astype(jnp.int32), axis=0)
    padded = ((counts + tm - 1) // tm) * tm
    start = jnp.cumsum(counts) - counts
    pcum = jnp.cumsum(padded)
    pstart = pcum - padded
    blk_start = jnp.arange(nblk, dtype=jnp.int32) * tm
    blk_expert = jnp.minimum(jnp.sum((pcum[None, :] <= blk_start[:, None]).astype(jnp.int32), axis=1),
                             n_experts - 1)
    rank = (blk_start - pstart[blk_expert])[:, None] + jnp.arange(tm, dtype=jnp.int32)[None, :]
    cnt = counts[blk_expert][:, None]
    valid = rank < cnt
    row_flat = order[jnp.minimum(start[blk_expert][:, None] + rank, tk - 1)]
    tok = row_flat // TOP_K
    kk = row_flat - tok * TOP_K
    src_tok = jnp.where(valid, tok, 0)
    dst_row = jnp.where(valid, kk * t + tok, tk + blk_expert[:, None] * tm + (rank - cnt))
    nact = (pcum[-1] // tm).astype(jnp.int32).reshape(1)
    return (blk_expert, nact, src_tok.reshape(nblk, 1, tm), dst_row.reshape(nblk, 1, tm), tk + n_experts * tm)


def kernel(x, mem, norm_mix_w, norm_mem_w, w_in, w_gate, b_gate, pool_w, pool_scale, q_norm_w, k_norm_w,
           lambda_q1, lambda_k1, lambda_q2, lambda_k2, diff_norm_w, rel_bias, w_mem_kv, mem_q_norm_w,
           mem_k_norm_w, w_up_pool, w_up_diff, w_up_mem, w_out, norm_ffn_w, router_w, router_b, w_gu, b_gu,
           w_down, b_down):
    batch, seq, d = x.shape
    mlen = mem.shape[1]
    t = batch * seq
    assert norm_mix_w.shape[0] == 1, "single-layer block"
    n_experts = router_w.shape[2]

    pool_width = len(POOL_WINDOWS) * POOL_GROUP_DIM
    qk_width = DIFF_HEADS * 2 * DIFF_QK_DIM
    v_width = DIFF_HEADS * DIFF_V_DIM
    mem_width = MEM_HEADS * MEM_HEAD_DIM
    q_col = pool_width
    k_col = q_col + qk_width
    v_col = k_col + qk_width
    mq_col = v_col + v_width
    chunk = 512
    assert w_in.shape[2] == mq_col + mem_width
    segs = (("p",) * (pool_width // chunk) + ("q",) * (qk_width // chunk) + ("k",) * (qk_width // chunk)
            + ("v",) * (v_width // chunk) + ("m",) * (mem_width // chunk))

    x2 = x.reshape(t, d)
    row = lambda a: a.reshape(1, -1)

    half = chunk // 2
    gmat = jnp.asarray(np.kron(np.eye(half // DIFF_QK_DIM), np.ones((DIFF_QK_DIM, DIFF_QK_DIM))), BF16)
    reps = chunk // DIFF_QK_DIM
    qw = row(jnp.tile(q_norm_w[0] * (1.0 / math.sqrt(DIFF_QK_DIM)), reps))
    kw = row(jnp.tile(k_norm_w[0], reps))
    mw = row(mem_q_norm_w[0] * (1.0 / math.sqrt(MEM_HEAD_DIM)))
    proj = _proj(x2, row(norm_mix_w[0]), w_in[0].astype(BF16), gmat, qw, kw, mw, tm=512, segs=segs, chunk=chunk)

    pool_out = _pool(proj, pool_w[0].astype(BF16), row(pool_scale[0]), batch=batch, seq=seq)

    rel_flat = rel_bias.reshape(-1)
    tables = _bias_tables(rel_flat, jnp.asarray(_bucket_tables(ATT_BLOCK)))
    lam_vecs = jnp.concatenate([lambda_q1, lambda_k1, lambda_q2, lambda_k2], axis=0)
    diff_out = _diff_attention(proj, tables, lam_vecs, row(diff_norm_w[0]),
                               batch=batch, seq=seq, q_col=q_col, k_col=k_col, v_col=v_col, hp=4)

    mk, mv = _mem_kv(mem.reshape(batch * mlen, d), row(norm_mem_w[0]), w_mem_kv[0].astype(BF16),
                     row(mem_k_norm_w[0]), batch=batch, mlen=mlen)
    mem_out = _mem_attention(proj, mk, mv, batch=batch, seq=seq, mlen=mlen, q_col=mq_col, tm=512)

    x1 = _merge(x2, row(norm_mix_w[0]), pool_out, diff_out, mem_out, w_gate[0].astype(BF16), row(b_gate[0]),
                w_up_pool[0].astype(BF16), w_up_diff[0].astype(BF16), w_up_mem[0].astype(BF16),
                w_out[0].astype(BF16), tm=512, tn=512)

    rw = router_w[0]
    rw_hi = rw.astype(BF16)
    rw_lo = (rw - rw_hi.astype(F32)).astype(BF16)
    idx_pad, gate_pad = _router(x1, row(norm_ffn_w[0]), rw_hi, rw_lo, row(router_b[0]), tm=512)
    tm_moe = 512
    blk_expert, nact, src_tok, dst_row, n_out_rows = _routing_plan(idx_pad[:, :TOP_K], n_experts, tm_moe)
    yk = _moe(blk_expert, nact, src_tok, dst_row, x1, row(norm_ffn_w[0]),
              w_gu[0], b_gu[0].reshape(n_experts, 1, -1),
              w_down[0], b_down[0].reshape(n_experts, 1, -1),
              tm=tm_moe, tf=512, n_out_rows=n_out_rows)
    out = _combine(x1, gate_pad, yk, tc=256)
    return out.reshape(batch, seq, d)
```
